```python
import math
import jax, jax.numpy as jnp
from jax import lax
import numpy as np

D_MODEL = 2048
BATCH = 2
SEQ = 8192
DEPTH = 2

POOL_WINDOWS = (2, 4, 8, 16)
POOL_GROUPS = 4
POOL_GROUP_DIM = D_MODEL // 8
POOL_WIDTH = POOL_GROUPS * POOL_GROUP_DIM

GDN_K_HEADS = 4
GDN_V_HEADS = 8
GDN_HEAD_DIM = 128
GDN_KEY_WIDTH = GDN_K_HEADS * GDN_HEAD_DIM
GDN_VAL_WIDTH = GDN_V_HEADS * GDN_HEAD_DIM
GDN_CONV_CH = 2 * GDN_KEY_WIDTH + GDN_VAL_WIDTH
GDN_CONV = 4
GDN_CHUNK = 64

CONF_WIDTH = D_MODEL // 2
CONF_CONV = 31

MLA_HEADS = 8
MLA_NOPE = 128
MLA_ROPE = 64
MLA_V = 128
MLA_Q_RANK = 512
MLA_KV_RANK = 512
ROPE_THETA = 10000.0
ATTN_BLOCK = 128

N_BRANCH = 4
FFN_DIM = 11 * D_MODEL // 4
FFN_CONV = 3
RMS_EPS = 1e-6
LN_EPS = 1e-5

IN_SPLITS = (
    POOL_WIDTH,
    GDN_KEY_WIDTH, GDN_KEY_WIDTH,
    GDN_VAL_WIDTH, GDN_VAL_WIDTH,
    GDN_V_HEADS, GDN_V_HEADS,
    2 * CONF_WIDTH,
    MLA_Q_RANK, MLA_KV_RANK,
    MLA_ROPE,
    N_BRANCH * D_MODEL,
)
IN_WIDTH = sum(IN_SPLITS)
IN_OFFSETS = tuple(int(o) for o in np.cumsum(IN_SPLITS)[:-1])

kernel_name = "hybrid_gated_pool_gdn_conformer_mla_block"


def rms_norm(x, g, eps=RMS_EPS):
    x32 = x.astype(jnp.float32)
    y = x32 * lax.rsqrt(jnp.mean(x32 * x32, axis=-1, keepdims=True) + eps)
    return (y * g.astype(jnp.float32)).astype(x.dtype)


def layer_norm(x, g, b, eps=LN_EPS):
    x32 = x.astype(jnp.float32)
    xc = x32 - jnp.mean(x32, axis=-1, keepdims=True)
    var = jnp.mean(xc * xc, axis=-1, keepdims=True)
    y = xc * lax.rsqrt(var + eps) * g.astype(jnp.float32) + b.astype(jnp.float32)
    return y.astype(x.dtype)


def l2norm(t):
    return t * lax.rsqrt(jnp.sum(t * t, axis=-1, keepdims=True) + 1e-6)


def causal_dwconv(x, w):
    width = w.shape[0]
    return lax.conv_general_dilated(
        x, w[:, None, :].astype(x.dtype), window_strides=(1,),
        padding=[(width - 1, 0)], dimension_numbers=('NWC', 'WIO', 'NWC'),
        feature_group_count=x.shape[-1])


def rope_cos_sin(positions):
    inv_freq = ROPE_THETA ** (-jnp.arange(0, MLA_ROPE, 2, dtype=jnp.float32) / MLA_ROPE)
    ang = positions.astype(jnp.float32)[..., None] * inv_freq
    return jnp.cos(ang), jnp.sin(ang)


def apply_rope(x, cos, sin):
    x32 = x.astype(jnp.float32)
    half = x.shape[-1] // 2
    x1, x2 = x32[..., :half], x32[..., half:]
    return jnp.concatenate([x1 * cos - x2 * sin, x2 * cos + x1 * sin], axis=-1).astype(x.dtype)


def pool_mixer(u, w_groups, scale):
    b, s, _ = u.shape
    ug = u.reshape(b, s, POOL_GROUPS, POOL_GROUP_DIM)
    csum = jnp.cumsum(ug.astype(jnp.float32), axis=1)
    t = jnp.arange(s)
    pooled = []
    for gi, win in enumerate(POOL_WINDOWS):
        c = csum[:, :, gi]
        lag = jnp.pad(c, ((0, 0), (win, 0), (0, 0)))[:, :s]
        cnt = jnp.minimum(t + 1, win).astype(jnp.float32)[None, :, None]
        pooled.append((c - lag) / cnt)
    diff = (jnp.stack(pooled, axis=2) - ug.astype(jnp.float32)).astype(u.dtype)
    y = jnp.einsum('bsgc,gcd->bsgd', diff, w_groups)
    return y.reshape(b, s, POOL_WIDTH) * scale


def chunk_gated_delta_rule(q, k, v, g, beta):
    b, s, h, dk = q.shape
    dv = v.shape[-1]
    c = GDN_CHUNK
    n = s // c

    def to_chunks(t):
        t = jnp.moveaxis(t.astype(jnp.float32), 2, 1)
        return t.reshape((b, h, n, c) + t.shape[3:])

    q = to_chunks(l2norm(q.astype(jnp.float32)) * (dk ** -0.5))
    k = to_chunks(l2norm(k.astype(jnp.float32)))
    v = to_chunks(v)
    beta = to_chunks(beta)
    g = jnp.cumsum(to_chunks(g), axis=-1)

    lower = jnp.tril(jnp.ones((c, c), dtype=bool))
    strict = jnp.tril(jnp.ones((c, c), dtype=bool), -1)
    gdiff = g[..., :, None] - g[..., None, :]
    decay = jnp.where(lower, jnp.exp(jnp.where(lower, gdiff, 0.0)), 0.0)

    k_beta = k * beta[..., None]
    v_beta = v * beta[..., None]
    lmat = jnp.where(strict, jnp.einsum('bhnid,bhnjd->bhnij', k_beta, k) * decay, 0.0)
    amat = lmat + jnp.eye(c, dtype=jnp.float32)
    rhs = jnp.concatenate([v_beta, k_beta * jnp.exp(g)[..., None]], axis=-1)
    sol = lax.linalg.triangular_solve(amat, rhs, left_side=True, lower=True,
                                      unit_diagonal=True)
    u, w = sol[..., :dv], sol[..., dv:]
    qk = jnp.einsum('bhnid,bhnjd->bhnij', q, k) * decay

    def step(state, xs):
        q_i, k_i, u_i, w_i, g_i, qk_i = xs
        v_new = u_i - jnp.einsum('bhck,bhkv->bhcv', w_i, state)
        o_i = (jnp.einsum('bhck,bhkv->bhcv', q_i * jnp.exp(g_i)[..., None], state)
               + jnp.einsum('bhij,bhjv->bhiv', qk_i, v_new))
        g_last = g_i[..., -1:]
        state = (state * jnp.exp(g_last)[..., None]
                 + jnp.einsum('bhck,bhcv->bhkv', k_i * jnp.exp(g_last - g_i)[..., None], v_new))
        return state, o_i

    xs = tuple(jnp.moveaxis(t, 2, 0) for t in (q, k, u, w, g, qk))
    state0 = jnp.zeros((b, h, dk, dv), jnp.float32)
    _, o = lax.scan(step, state0, xs)
    o = jnp.moveaxis(o, 0, 2).reshape(b, h, s, dv)
    return jnp.moveaxis(o, 1, 2)


def gated_deltanet(q, k, v, z, a, bb, conv_w, a_log, dt_bias, norm_g):
    b, s, _ = q.shape
    qkv = jax.nn.silu(causal_dwconv(jnp.concatenate([q, k, v], axis=-1), conv_w))
    q, k, v = jnp.split(qkv, [GDN_KEY_WIDTH, 2 * GDN_KEY_WIDTH], axis=-1)
    rep = GDN_V_HEADS // GDN_K_HEADS
    q = jnp.repeat(q.reshape(b, s, GDN_K_HEADS, GDN_HEAD_DIM), rep, axis=2)
    k = jnp.repeat(k.reshape(b, s, GDN_K_HEADS, GDN_HEAD_DIM), rep, axis=2)
    v = v.reshape(b, s, GDN_V_HEADS, GDN_HEAD_DIM)
    beta = jax.nn.sigmoid(bb.astype(jnp.float32))
    g = -jnp.exp(a_log.astype(jnp.float32)) * jax.nn.softplus(
        a.astype(jnp.float32) + dt_bias.astype(jnp.float32))
    o = chunk_gated_delta_rule(q, k, v, g, beta)
    zf = z.reshape(b, s, GDN_V_HEADS, GDN_HEAD_DIM).astype(jnp.float32)
    o = rms_norm(o, norm_g) * jax.nn.silu(zf)
    return o.reshape(b, s, GDN_VAL_WIDTH).astype(z.dtype)


def conformer_conv(u, conv_w, conv_b, ln_g, ln_b):
    a, gate = jnp.split(u, 2, axis=-1)
    h = a * jax.nn.sigmoid(gate)
    h = causal_dwconv(h, conv_w) + conv_b
    h = layer_norm(h, ln_g, ln_b)
    return jax.nn.silu(h)


def causal_block_attention(q, k, v):
    b, s, h, dqk = q.shape
    nb = s // ATTN_BLOCK
    scale = dqk ** -0.5
    qb = jnp.moveaxis(q.reshape(b, nb, ATTN_BLOCK, h, dqk), 1, 0)
    k_idx = jnp.arange(s)

    def one_block(args):
        q_i, bi = args
        sc = jnp.einsum('bqhd,bkhd->bhqk', q_i, k).astype(jnp.float32) * scale
        q_idx = bi * ATTN_BLOCK + jnp.arange(ATTN_BLOCK)
        sc = jnp.where(k_idx[None, :] <= q_idx[:, None], sc, -jnp.inf)
        p = jax.nn.softmax(sc, axis=-1).astype(v.dtype)
        return jnp.einsum('bhqk,bkhd->bqhd', p, v)

    o = lax.map(one_block, (qb, jnp.arange(nb)))
    return jnp.moveaxis(o, 0, 1).reshape(b, s, h, v.shape[-1])


def mla(c_q, c_kv, k_rope, cos, sin, q_norm, w_uq, kv_norm, w_ukv):
    b, s, _ = c_q.shape
    q = (rms_norm(c_q, q_norm) @ w_uq).reshape(b, s, MLA_HEADS, MLA_NOPE + MLA_ROPE)
    kv = (rms_norm(c_kv, kv_norm) @ w_ukv).reshape(b, s, MLA_HEADS, MLA_NOPE + MLA_V)
    q_pe = apply_rope(q[..., MLA_NOPE:], cos[:, :, None], sin[:, :, None])
    k_pe = apply_rope(k_rope, cos, sin)
    q = jnp.concatenate([q[..., :MLA_NOPE], q_pe], axis=-1)
    k = jnp.concatenate([kv[..., :MLA_NOPE],
                         jnp.broadcast_to(k_pe[:, :, None, :], (b, s, MLA_HEADS, MLA_ROPE))], axis=-1)
    o = causal_block_attention(q, k, kv[..., MLA_NOPE:])
    return o.reshape(b, s, MLA_HEADS * MLA_V)


def hybrid_mixer(xn, cos, sin, w_in, pool_w, pool_scale, gdn_conv_w, gdn_a_log, gdn_dt_bias,
                 gdn_norm, conf_conv_w, conf_conv_b, conf_ln_g, conf_ln_b, mla_q_norm,
                 mla_w_uq, mla_kv_norm, mla_w_ukv, w_pool_out, w_gdn_out, w_conf_out,
                 w_mla_out, w_out):
    b, s, _ = xn.shape
    proj = xn @ w_in
    (u_pool, q_g, k_g, v_g, z_g, a_g, b_g, u_conf, c_q, c_kv, k_rope,
     gate_logits) = jnp.split(proj, IN_OFFSETS, axis=-1)
    y_a = pool_mixer(u_pool, pool_w, pool_scale) @ w_pool_out
    y_b = gated_deltanet(q_g, k_g, v_g, z_g, a_g, b_g, gdn_conv_w, gdn_a_log,
                         gdn_dt_bias, gdn_norm) @ w_gdn_out
    y_c = conformer_conv(u_conf, conf_conv_w, conf_conv_b, conf_ln_g, conf_ln_b) @ w_conf_out
    y_d = mla(c_q, c_kv, k_rope, cos, sin, mla_q_norm, mla_w_uq, mla_kv_norm,
              mla_w_ukv) @ w_mla_out
    gates = jax.nn.sigmoid(gate_logits.astype(jnp.float32)).reshape(
        b, s, N_BRANCH, D_MODEL).astype(xn.dtype)
    merged = (gates[:, :, 0] * y_a + gates[:, :, 1] * y_b
              + gates[:, :, 2] * y_c + gates[:, :, 3] * y_d)
    return merged @ w_out


def conv_ffn(xn, w_up, conv_w, conv_b, w_down):
    h = causal_dwconv(xn @ w_up, conv_w) + conv_b
    gate, up = jnp.split(h, 2, axis=-1)
    return (jax.nn.silu(gate) * up) @ w_down


def setup_inputs(seed: int = 0) -> dict:
    key = jax.random.key(seed)
    ks = iter(jax.random.split(key, 40))
    L = DEPTH
    f32 = jnp.float32

    def nrm(shape, fan_in):
        return jax.random.normal(next(ks), shape, f32) * (fan_in ** -0.5)

    def gain(shape):
        return 1.0 + 0.02 * jax.random.normal(next(ks), shape, f32)

    def small(shape):
        return 0.02 * jax.random.normal(next(ks), shape, f32)

    x = jax.random.normal(next(ks), (BATCH, SEQ, D_MODEL), f32)
    offset = jax.random.randint(next(ks), (BATCH, 1), 0, 1024, jnp.int32)
    positions = offset + jnp.arange(SEQ, dtype=jnp.int32)[None, :]
    mix_norm = gain((L, D_MODEL))
    w_in = nrm((L, D_MODEL, IN_WIDTH), D_MODEL)
    pool_w = nrm((L, POOL_GROUPS, POOL_GROUP_DIM, POOL_GROUP_DIM), POOL_GROUP_DIM)
    pool_scale = gain((L, POOL_WIDTH))
    gdn_conv_w = nrm((L, GDN_CONV, GDN_CONV_CH), GDN_CONV)
    gdn_a_log = jnp.log(jax.random.uniform(next(ks), (L, GDN_V_HEADS), f32, 1.0, 16.0))
    dt = jnp.exp(jax.random.uniform(next(ks), (L, GDN_V_HEADS), f32,
                                    math.log(1e-3), math.log(1e-1)))
    gdn_dt_bias = dt + jnp.log(-jnp.expm1(-dt))
    gdn_norm = gain((L, GDN_HEAD_DIM))
    conf_conv_w = nrm((L, CONF_CONV, CONF_WIDTH), CONF_CONV)
    conf_conv_b = small((L, CONF_WIDTH))
    conf_ln_g = gain((L, CONF_WIDTH))
    conf_ln_b = small((L, CONF_WIDTH))
    mla_q_norm = gain((L, MLA_Q_RANK))
    mla_w_uq = nrm((L, MLA_Q_RANK, MLA_HEADS * (MLA_NOPE + MLA_ROPE)), MLA_Q_RANK)
    mla_kv_norm = gain((L, MLA_KV_RANK))
    mla_w_ukv = nrm((L, MLA_KV_RANK, MLA_HEADS * (MLA_NOPE + MLA_V)), MLA_KV_RANK)
    w_pool_out = nrm((L, POOL_WIDTH, D_MODEL), POOL_WIDTH)
    w_gdn_out = nrm((L, GDN_VAL_WIDTH, D_MODEL), GDN_VAL_WIDTH)
    w_conf_out = nrm((L, CONF_WIDTH, D_MODEL), CONF_WIDTH)
    w_mla_out = nrm((L, MLA_HEADS * MLA_V, D_MODEL), MLA_HEADS * MLA_V)
    w_out = nrm((L, D_MODEL, D_MODEL), D_MODEL)
    ffn_norm = gain((L, D_MODEL))
    ffn_w_up = nrm((L, D_MODEL, 2 * FFN_DIM), D_MODEL)
    ffn_conv_w = nrm((L, FFN_CONV, 2 * FFN_DIM), FFN_CONV)
    ffn_conv_b = small((L, 2 * FFN_DIM))
    ffn_w_down = nrm((L, FFN_DIM, D_MODEL), FFN_DIM)
    final_norm = gain((D_MODEL,))
    return {
        'x': x, 'positions': positions, 'mix_norm': mix_norm, 'w_in': w_in,
        'pool_w': pool_w, 'pool_scale': pool_scale, 'gdn_conv_w': gdn_conv_w,
        'gdn_a_log': gdn_a_log, 'gdn_dt_bias': gdn_dt_bias, 'gdn_norm': gdn_norm,
        'conf_conv_w': conf_conv_w, 'conf_conv_b': conf_conv_b, 'conf_ln_g': conf_ln_g,
        'conf_ln_b': conf_ln_b, 'mla_q_norm': mla_q_norm, 'mla_w_uq': mla_w_uq,
        'mla_kv_norm': mla_kv_norm, 'mla_w_ukv': mla_w_ukv, 'w_pool_out': w_pool_out,
        'w_gdn_out': w_gdn_out, 'w_conf_out': w_conf_out, 'w_mla_out': w_mla_out,
        'w_out': w_out, 'ffn_norm': ffn_norm, 'ffn_w_up': ffn_w_up,
        'ffn_conv_w': ffn_conv_w, 'ffn_conv_b': ffn_conv_b, 'ffn_w_down': ffn_w_down,
        'final_norm': final_norm,
    }


def reference(x, positions, mix_norm, w_in, pool_w, pool_scale, gdn_conv_w, gdn_a_log,
              gdn_dt_bias, gdn_norm, conf_conv_w, conf_conv_b, conf_ln_g, conf_ln_b,
              mla_q_norm, mla_w_uq, mla_kv_norm, mla_w_ukv, w_pool_out, w_gdn_out,
              w_conf_out, w_mla_out, w_out, ffn_norm, ffn_w_up, ffn_conv_w, ffn_conv_b,
              ffn_w_down, final_norm):
    cos, sin = rope_cos_sin(positions)
    for l in range(DEPTH):
        xn = rms_norm(x, mix_norm[l])
        x = x + hybrid_mixer(xn, cos, sin, w_in[l], pool_w[l], pool_scale[l], gdn_conv_w[l],
                             gdn_a_log[l], gdn_dt_bias[l], gdn_norm[l], conf_conv_w[l],
                             conf_conv_b[l], conf_ln_g[l], conf_ln_b[l], mla_q_norm[l],
                             mla_w_uq[l], mla_kv_norm[l], mla_w_ukv[l], w_pool_out[l],
                             w_gdn_out[l], w_conf_out[l], w_mla_out[l], w_out[l])
        hn = rms_norm(x, ffn_norm[l])
        x = x + conv_ffn(hn, ffn_w_up[l], ffn_conv_w[l], ffn_conv_b[l], ffn_w_down[l])
    return rms_norm(x, final_norm)
```

```python
import functools
import math

import numpy as np
import jax
import jax.numpy as jnp
from jax import lax
from jax.experimental import pallas as pl
from jax.experimental.pallas import tpu as pltpu

F32 = jnp.float32
BF16 = jnp.bfloat16

D_MODEL = 2048
DEPTH = 2
POOL_WINDOWS = (2, 4, 8, 16)
POOL_GROUPS = 4
POOL_GROUP_DIM = 256
POOL_WIDTH = 1024
GDN_K_HEADS = 4
GDN_V_HEADS = 8
GDN_HEAD_DIM = 128
GDN_KEY_WIDTH = 512
GDN_VAL_WIDTH = 1024
GDN_CONV_CH = 2048
GDN_CONV = 4
GDN_CHUNK = 64
CONF_WIDTH = 1024
CONF_CONV = 31
MLA_HEADS = 8
MLA_NOPE = 128
MLA_ROPE = 64
MLA_V = 128
MLA_QK = MLA_NOPE + MLA_ROPE
MLA_Q_RANK = 512
MLA_KV_RANK = 512
ROPE_THETA = 10000.0
N_BRANCH = 4
FFN_DIM = 5632
FFN_CONV = 3
RMS_EPS = 1e-6
LN_EPS = 1e-5

VMEM_LIMIT_BYTES = 56 * 1024 * 1024
HALO = 8
POOL_HALO = 16
CONF_HALO = 32

PA_QKV, PA_CONF, PA_POOL, PA_Z, PA_CQ, PA_CKV, PA_GATE = 0, 2048, 4096, 5120, 6144, 6656, 7168
PA_WIDTH = PA_GATE + N_BRANCH * D_MODEL
PC_WIDTH = 384


def _cparams(semantics):
    return pltpu.CompilerParams(dimension_semantics=semantics, vmem_limit_bytes=VMEM_LIMIT_BYTES)


def _sigmoid(x):
    return jax.nn.sigmoid(x)


def _silu(x):
    return x * jax.nn.sigmoid(x)


def _dot(a, b):
    return jnp.dot(a, b, preferred_element_type=F32)


def _dot_nt(a, b):
    return lax.dot_general(a, b, (((1,), (1,)), ((), ())), preferred_element_type=F32)


def _rmsnorm_kernel(x_ref, g_ref, o_ref):
    x = x_ref[...]
    ms = jnp.mean(x * x, axis=-1, keepdims=True)
    o_ref[...] = (x * lax.rsqrt(ms + RMS_EPS) * g_ref[...]).astype(o_ref.dtype)


def rmsnorm(x, g, out_dtype, tm=512):
    m, d = x.shape
    return pl.pallas_call(
        _rmsnorm_kernel,
        grid=(m // tm,),
        in_specs=[pl.BlockSpec((tm, d), lambda i: (i, 0)), pl.BlockSpec((1, d), lambda i: (0, 0))],
        out_specs=pl.BlockSpec((tm, d), lambda i: (i, 0)),
        out_shape=jax.ShapeDtypeStruct((m, d), out_dtype),
        compiler_params=_cparams(("arbitrary",)),
        name="rmsnorm",
    )(x, g.reshape(1, d).astype(F32))


def _mm_kernel(*refs, nk, sigmoid_from, has_res):
    if has_res:
        a_ref, w_ref, r_ref, o_ref, acc_ref = refs
    else:
        a_ref, w_ref, o_ref, acc_ref = refs
        r_ref = None
    j = pl.program_id(1)
    k = pl.program_id(2)
    part = _dot(a_ref[...], w_ref[...])

    def finish(acc):
        if r_ref is not None:
            acc = acc + r_ref[...]
        if sigmoid_from is None:
            o_ref[...] = acc.astype(o_ref.dtype)
        else:
            @pl.when(j >= sigmoid_from)
            def _():
                o_ref[...] = _sigmoid(acc).astype(o_ref.dtype)

            @pl.when(j < sigmoid_from)
            def _():
                o_ref[...] = acc.astype(o_ref.dtype)

    if nk == 1:
        finish(part)
    else:
        @pl.when(k == 0)
        def _():
            acc_ref[...] = part

        @pl.when(jnp.logical_and(k > 0, k < nk - 1))
        def _():
            acc_ref[...] += part

        @pl.when(k == nk - 1)
        def _():
            finish(acc_ref[...] + part)


def matmul(a, w, *, tm, tn, tk, out_dtype, sigmoid_from=None, residual=None, name="matmul"):
    m, kdim = a.shape
    n = w.shape[1]
    nk = kdim // tk
    in_specs = [pl.BlockSpec((tm, tk), lambda i, j, k: (i, k)),
                pl.BlockSpec((tk, tn), lambda i, j, k: (k, j))]
    args = [a, w]
    if residual is not None:
        in_specs.append(pl.BlockSpec((tm, tn), lambda i, j, k: (i, j)))
        args.append(residual)
    return pl.pallas_call(
        functools.partial(_mm_kernel, nk=nk, sigmoid_from=sigmoid_from, has_res=residual is not None),
        grid=(m // tm, n // tn, nk),
        in_specs=in_specs,
        out_specs=pl.BlockSpec((tm, tn), lambda i, j, k: (i, j)),
        out_shape=jax.ShapeDtypeStruct((m, n), out_dtype),
        scratch_shapes=[pltpu.VMEM((tm, tn) if nk > 1 else (8, 128), F32)],
        compiler_params=_cparams(("arbitrary", "arbitrary", "arbitrary")),
        name=name,
    )(*args)


def _pool_bands(ts):
    i = np.arange(ts)[:, None]
    m = np.arange(ts + POOL_HALO)[None, :]
    lag = (i + POOL_HALO) - m
    return np.stack([((lag >= 0) & (lag < w)) for w in POOL_WINDOWS]).astype(np.float32)


def _pool_kernel(u_ref, band_ref, pw_ref, scale_ref, o_ref, tail_ref, *, ts):
    s = pl.program_id(1)

    @pl.when(s == 0)
    def _():
        tail_ref[...] = jnp.zeros_like(tail_ref)

    t_abs = s * ts + lax.broadcasted_iota(jnp.int32, (ts, 1), 0)
    for g, win_len in enumerate(POOL_WINDOWS):
        cols = slice(g * POOL_GROUP_DIM, (g + 1) * POOL_GROUP_DIM)
        ug = u_ref[:, cols]
        win = jnp.concatenate([tail_ref[:, cols], ug], axis=0)
        sums = _dot(band_ref[g], win)
        cnt = jnp.minimum(t_abs + 1, win_len).astype(F32)
        diff = sums / cnt - ug.astype(F32)
        y = _dot(diff.astype(BF16), pw_ref[g]) * scale_ref[:, cols]
        o_ref[:, cols] = y.astype(o_ref.dtype)
    tail_ref[...] = u_ref[ts - POOL_HALO:, :]


def pool_mixer(proj_a, pool_w, pool_scale, *, batch, seq, ts=512):
    ns = seq // ts
    bands = jnp.asarray(_pool_bands(ts), BF16)
    cb = PA_POOL // POOL_WIDTH
    return pl.pallas_call(
        functools.partial(_pool_kernel, ts=ts),
        grid=(batch, ns),
        in_specs=[pl.BlockSpec((ts, POOL_WIDTH), lambda b, s: (b * ns + s, cb)),
                  pl.BlockSpec((POOL_GROUPS, ts, ts + POOL_HALO), lambda b, s: (0, 0, 0)),
                  pl.BlockSpec((POOL_GROUPS, POOL_GROUP_DIM, POOL_GROUP_DIM), lambda b, s: (0, 0, 0)),
                  pl.BlockSpec((1, POOL_WIDTH), lambda b, s: (0, 0))],
        out_specs=pl.BlockSpec((ts, POOL_WIDTH), lambda b, s: (b * ns + s, 0)),
        out_shape=jax.ShapeDtypeStruct((batch * seq, POOL_WIDTH), BF16),
        scratch_shapes=[pltpu.VMEM((POOL_HALO, POOL_WIDTH), BF16)],
        compiler_params=_cparams(("arbitrary", "arbitrary")),
        name="pool_mixer",
    )(proj_a, bands, pool_w.astype(BF16), pool_scale.reshape(1, POOL_WIDTH).astype(F32))


def _conf_kernel(u_ref, cw_ref, cb_ref, lg_ref, lb_ref, o_ref, win_ref, *, ts, rc):
    s = pl.program_id(1)

    @pl.when(s == 0)
    def _():
        win_ref[0:CONF_HALO, :] = jnp.zeros((CONF_HALO, CONF_WIDTH), F32)

    a = u_ref[:, :CONF_WIDTH].astype(F32)
    gate = u_ref[:, CONF_WIDTH:].astype(F32)
    win_ref[CONF_HALO:CONF_HALO + ts, :] = a * _sigmoid(gate)
    off = CONF_HALO - (CONF_CONV - 1)
    for r in range(ts // rc):
        base = r * rc
        acc = jnp.zeros((rc, CONF_WIDTH), F32)
        for j in range(CONF_CONV):
            acc = acc + win_ref[base + off + j:base + off + j + rc, :] * cw_ref[j:j + 1, :]
        h = acc + cb_ref[...]
        mu = jnp.mean(h, axis=-1, keepdims=True)
        hc = h - mu
        var = jnp.mean(hc * hc, axis=-1, keepdims=True)
        y = hc * lax.rsqrt(var + LN_EPS) * lg_ref[...] + lb_ref[...]
        o_ref[base:base + rc, :] = _silu(y).astype(o_ref.dtype)
    win_ref[0:CONF_HALO, :] = win_ref[ts:ts + CONF_HALO, :]


def conformer_conv(proj_a, conv_w, conv_b, ln_g, ln_b, *, batch, seq, ts=256, rc=16):
    ns = seq // ts
    cw = jnp.zeros((32, CONF_WIDTH), F32).at[:CONF_CONV].set(conv_w.astype(F32))
    row = lambda v: v.reshape(1, CONF_WIDTH).astype(F32)
    cb = PA_CONF // (2 * CONF_WIDTH)
    const = lambda b, s: (0, 0)
    return pl.pallas_call(
        functools.partial(_conf_kernel, ts=ts, rc=rc),
        grid=(batch, ns),
        in_specs=[pl.BlockSpec((ts, 2 * CONF_WIDTH), lambda b, s: (b * ns + s, cb)),
                  pl.BlockSpec((32, CONF_WIDTH), const),
                  pl.BlockSpec((1, CONF_WIDTH), const),
                  pl.BlockSpec((1, CONF_WIDTH), const),
                  pl.BlockSpec((1, CONF_WIDTH), const)],
        out_specs=pl.BlockSpec((ts, CONF_WIDTH), lambda b, s: (b * ns + s, 0)),
        out_shape=jax.ShapeDtypeStruct((batch * seq, CONF_WIDTH), BF16),
        scratch_shapes=[pltpu.VMEM((ts + CONF_HALO, CONF_WIDTH), F32)],
        compiler_params=_cparams(("arbitrary", "arbitrary")),
        name="conformer_conv",
    )(proj_a, cw, row(conv_b), row(ln_g), row(ln_b))


def _split_dot(a, b):
    a_hi = a.astype(BF16)
    b_hi = b.astype(BF16)
    a_lo = (a - a_hi.astype(F32)).astype(BF16)
    b_lo = (b - b_hi.astype(F32)).astype(BF16)
    return _dot(a_hi, b_hi) + (_dot(a_hi, b_lo) + _dot(a_lo, b_hi))


def _gdn_kernel(qkv_ref, z_ref, ab_ref, cw_ref, alog_ref, dtb_ref, ng_ref, o_ref, win_ref, state_ref):
    c = GDN_CHUNK
    dh = GDN_HEAD_DIM
    step = pl.program_id(1)

    @pl.when(step == 0)
    def _():
        win_ref[0:HALO, :] = jnp.zeros((HALO, GDN_CONV_CH), F32)
        state_ref[...] = jnp.zeros_like(state_ref)

    win_ref[HALO:HALO + c, :] = qkv_ref[...].astype(F32)
    off = HALO - (GDN_CONV - 1)
    y = jnp.zeros((c, GDN_CONV_CH), F32)
    for j in range(GDN_CONV):
        y = y + win_ref[off + j:off + j + c, :] * cw_ref[j:j + 1, :]
    y = _silu(y)
    win_ref[0:HALO, :] = win_ref[c:c + HALO, :]

    ab = ab_ref[...]
    g_all = -jnp.exp(alog_ref[...]) * jnp.logaddexp(ab + dtb_ref[...], 0.0)
    ii = lax.broadcasted_iota(jnp.int32, (c, c), 0)
    jj = lax.broadcasted_iota(jnp.int32, (c, c), 1)
    lower = ii >= jj
    strict = ii > jj
    tri = lower.astype(F32)
    eye = (ii == jj).astype(F32)
    gc_all = jnp.dot(tri, g_all, precision=lax.Precision.HIGHEST, preferred_element_type=F32)
    gc_t = gc_all.T
    beta_all = _sigmoid(ab)

    def l2n(t):
        return t * lax.rsqrt(jnp.sum(t * t, axis=-1, keepdims=True) + 1e-6)

    q_heads, k_heads, kt_heads = [], [], []
    for kh in range(GDN_K_HEADS):
        qh = l2n(y[:, kh * dh:(kh + 1) * dh]) * (dh ** -0.5)
        kh_ = l2n(y[:, GDN_KEY_WIDTH + kh * dh:GDN_KEY_WIDTH + (kh + 1) * dh])
        q_heads.append(qh)
        k_heads.append(kh_)
        kt_heads.append(kh_.T.astype(BF16))

    for h in range(GDN_V_HEADS):
        kh = h // (GDN_V_HEADS // GDN_K_HEADS)
        q, k, kt = q_heads[kh], k_heads[kh], kt_heads[kh]
        v = y[:, 2 * GDN_KEY_WIDTH + h * dh:2 * GDN_KEY_WIDTH + (h + 1) * dh]
        gcol = gc_all[:, h:h + 1]
        grow = gc_t[h:h + 1, :]
        beta = beta_all[:, 8 + h:9 + h]
        gdiff = gcol - grow
        decay = jnp.where(lower, jnp.exp(jnp.where(lower, gdiff, 0.0)), 0.0)
        kb = k * beta
        vb = v * beta
        k16 = k.astype(BF16)
        lmat = jnp.where(strict, _dot_nt(kb.astype(BF16), k16) * decay, 0.0)
        qk = _dot_nt(q.astype(BF16), k16) * decay
        tinv = eye - lmat
        pw = lmat
        for _ in range(5):
            pw = _split_dot(pw, pw)
            tinv = tinv + _split_dot(tinv, pw)
        eg = jnp.exp(gcol)
        u = _split_dot(tinv, vb)
        w = _split_dot(tinv, kb * eg)
        s_old = state_ref[h]
        s16 = s_old.astype(BF16)
        v_new = u - _dot(w.astype(BF16), s16)
        o = _dot((q * eg).astype(BF16), s16) + _dot(qk.astype(BF16), v_new.astype(BF16))
        glast = gcol[c - 1:c, :]
        vd = (v_new * jnp.exp(glast - gcol)).astype(BF16)
        state_ref[h] = s_old * jnp.exp(glast) + _dot(kt, vd)
        zh = z_ref[:, h * dh:(h + 1) * dh].astype(F32)
        on = o * lax.rsqrt(jnp.mean(o * o, axis=-1, keepdims=True) + RMS_EPS) * ng_ref[...]
        o_ref[:, h * dh:(h + 1) * dh] = (on * _silu(zh)).astype(o_ref.dtype)


def gated_deltanet(proj_a, proj_c, conv_w, a_log, dt_bias, norm_g, *, batch, seq):
    c = GDN_CHUNK
    nc = seq // c
    cw = jnp.zeros((8, GDN_CONV_CH), F32).at[:GDN_CONV].set(conv_w.astype(F32))
    pad = lambda v: jnp.zeros((1, 128), F32).at[0, :GDN_V_HEADS].set(v.astype(F32))
    const = lambda b, s: (0, 0)
    return pl.pallas_call(
        _gdn_kernel,
        grid=(batch, nc),
        in_specs=[pl.BlockSpec((c, GDN_CONV_CH), lambda b, s: (b * nc + s, PA_QKV // GDN_CONV_CH)),
                  pl.BlockSpec((c, GDN_VAL_WIDTH), lambda b, s: (b * nc + s, PA_Z // GDN_VAL_WIDTH)),
                  pl.BlockSpec((c, 128), lambda b, s: (b * nc + s, 0)),
                  pl.BlockSpec((8, GDN_CONV_CH), const),
                  pl.BlockSpec((1, 128), const),
                  pl.BlockSpec((1, 128), const),
                  pl.BlockSpec((1, GDN_HEAD_DIM), const)],
        out_specs=pl.BlockSpec((c, GDN_VAL_WIDTH), lambda b, s: (b * nc + s, 0)),
        out_shape=jax.ShapeDtypeStruct((batch * seq, GDN_VAL_WIDTH), BF16),
        scratch_shapes=[pltpu.VMEM((c + HALO, GDN_CONV_CH), F32),
                        pltpu.VMEM((GDN_V_HEADS, GDN_HEAD_DIM, GDN_HEAD_DIM), F32)],
        compiler_params=_cparams(("arbitrary", "arbitrary")),
        name="gated_deltanet",
    )(proj_a, proj_a, proj_c, cw, pad(a_log), pad(dt_bias), norm_g.reshape(1, GDN_HEAD_DIM).astype(F32))


def _rope_kernel(pos_ref, freq_ref, sign_ref, c_ref, s_ref):
    ang = pos_ref[...].astype(F32) * freq_ref[...]
    c_ref[...] = jnp.cos(ang)
    s_ref[...] = jnp.sin(ang) * sign_ref[...]


def rope_tables(positions, tm=512):
    t = positions.size
    inv_freq = ROPE_THETA ** (-jnp.arange(0, MLA_ROPE, 2, dtype=F32) / MLA_ROPE)
    freq = jnp.concatenate([jnp.zeros((MLA_NOPE,), F32), inv_freq, inv_freq]).reshape(1, MLA_QK)
    half = MLA_ROPE // 2
    sign = jnp.concatenate([jnp.zeros((MLA_NOPE,), F32), -jnp.ones((half,), F32),
                            jnp.ones((half,), F32)]).reshape(1, MLA_QK)
    return pl.pallas_call(
        _rope_kernel,
        grid=(t // tm,),
        in_specs=[pl.BlockSpec((tm, 1), lambda i: (i, 0)),
                  pl.BlockSpec((1, MLA_QK), lambda i: (0, 0)),
                  pl.BlockSpec((1, MLA_QK), lambda i: (0, 0))],
        out_specs=[pl.BlockSpec((tm, MLA_QK), lambda i: (i, 0))] * 2,
        out_shape=[jax.ShapeDtypeStruct((t, MLA_QK), F32)] * 2,
        compiler_params=_cparams(("arbitrary",)),
        name="rope_tables",
    )(positions.reshape(t, 1).astype(jnp.int32), freq, sign)


def _mla_prep_kernel(cq_ref, ckv_ref, kr_ref, krs_ref, cos_ref, sin_ref, qn_ref, kvn_ref,
                     wq_ref, wqs_ref, wk_ref, wv_ref, q_out, k_out, v_out, cqn_ref, ckvn_ref, kpe_ref):
    h = pl.program_id(1)

    @pl.when(h == 0)
    def _():
        def rms(x, g):
            return x * lax.rsqrt(jnp.mean(x * x, axis=-1, keepdims=True) + RMS_EPS) * g
        cqn_ref[...] = rms(cq_ref[...].astype(F32), qn_ref[...]).astype(BF16)
        ckvn_ref[...] = rms(ckv_ref[...].astype(F32), kvn_ref[...]).astype(BF16)
        kpe = (kr_ref[:, :MLA_ROPE] * cos_ref[:, MLA_NOPE:]
               + krs_ref[:, :MLA_ROPE] * sin_ref[:, MLA_NOPE:])
        kpe_ref[...] = kpe.astype(BF16)

    cqn = cqn_ref[...]
    ckvn = ckvn_ref[...]
    q = _dot(cqn, wq_ref[0]) * cos_ref[...] + _dot(cqn, wqs_ref[0]) * sin_ref[...]
    q_out[0] = (q * (MLA_QK ** -0.5)).astype(q_out.dtype)
    k_out[0, :, :MLA_NOPE] = _dot(ckvn, wk_ref[0]).astype(k_out.dtype)
    k_out[0, :, MLA_NOPE:] = kpe_ref[...]
    v_out[0] = _dot(ckvn, wv_ref[0]).astype(v_out.dtype)


def mla_prep(proj_a, proj_c, cos_t, sin_t, q_norm, kv_norm, w_uq, w_ukv, tm=512):
    t = proj_a.shape[0]
    hh = MLA_HEADS
    wq = w_uq.reshape(MLA_Q_RANK, hh, MLA_QK)
    half = MLA_ROPE // 2
    wq_swap = jnp.concatenate([jnp.zeros((MLA_Q_RANK, hh, MLA_NOPE), w_uq.dtype),
                               wq[:, :, MLA_NOPE + half:], wq[:, :, MLA_NOPE:MLA_NOPE + half]], axis=-1)
    wkv = w_ukv.reshape(MLA_KV_RANK, hh, MLA_NOPE + MLA_V)
    to_heads = lambda w: jnp.transpose(w, (1, 0, 2)).astype(BF16)
    row = lambda v: v.reshape(1, -1).astype(F32)
    const = lambda i, h: (0, 0)
    per_head = lambda i, h: (h, 0, 0)
    return pl.pallas_call(
        _mla_prep_kernel,
        grid=(t // tm, hh),
        in_specs=[pl.BlockSpec((tm, MLA_Q_RANK), lambda i, h: (i, PA_CQ // MLA_Q_RANK)),
                  pl.BlockSpec((tm, MLA_KV_RANK), lambda i, h: (i, PA_CKV // MLA_KV_RANK)),
                  pl.BlockSpec((tm, 128), lambda i, h: (i, 1)),
                  pl.BlockSpec((tm, 128), lambda i, h: (i, 2)),
                  pl.BlockSpec((tm, MLA_QK), lambda i, h: (i, 0)),
                  pl.BlockSpec((tm, MLA_QK), lambda i, h: (i, 0)),
                  pl.BlockSpec((1, MLA_Q_RANK), const),
                  pl.BlockSpec((1, MLA_KV_RANK), const),
                  pl.BlockSpec((1, MLA_Q_RANK, MLA_QK), per_head),
                  pl.BlockSpec((1, MLA_Q_RANK, MLA_QK), per_head),
                  pl.BlockSpec((1, MLA_KV_RANK, MLA_NOPE), per_head),
                  pl.BlockSpec((1, MLA_KV_RANK, MLA_V), per_head)],
        out_specs=[pl.BlockSpec((1, tm, MLA_QK), lambda i, h: (h, i, 0)),
                   pl.BlockSpec((1, tm, MLA_QK), lambda i, h: (h, i, 0)),
                   pl.BlockSpec((1, tm, MLA_V), lambda i, h: (h, i, 0))],
        out_shape=[jax.ShapeDtypeStruct((hh, t, MLA_QK), BF16),
                   jax.ShapeDtypeStruct((hh, t, MLA_QK), BF16),
                   jax.ShapeDtypeStruct((hh, t, MLA_V), BF16)],
        scratch_shapes=[pltpu.VMEM((tm, MLA_Q_RANK), BF16), pltpu.VMEM((tm, MLA_KV_RANK), BF16),
                        pltpu.VMEM((tm, MLA_ROPE), BF16)],
        compiler_params=_cparams(("arbitrary", "arbitrary")),
        name="mla_prep",
    )(proj_a, proj_a, proj_c, proj_c, cos_t, sin_t, row(q_norm), row(kv_norm),
      to_heads(wq), to_heads(wq_swap), to_heads(wkv[:, :, :MLA_NOPE]), to_heads(wkv[:, :, MLA_NOPE:]))


def _flash_kernel(qi_tab, ki_tab, q_ref, k_ref, v_ref, o_ref, m_ref, l_ref, acc_ref, *, tq, tk):
    p = pl.program_id(2)
    qi = qi_tab[p]
    ki = ki_tab[p]

    @pl.when(ki == 0)
    def _():
        m_ref[...] = jnp.full_like(m_ref, -jnp.inf)
        l_ref[...] = jnp.zeros_like(l_ref)
        acc_ref[...] = jnp.zeros_like(acc_ref)

    s = _dot_nt(q_ref[0], k_ref[0])

    def update(sc):
        m_old = m_ref[...]
        m_new = jnp.maximum(m_old, jnp.max(sc, axis=-1, keepdims=True))
        alpha = jnp.exp(m_old - m_new)
        pr = jnp.exp(sc - m_new)
        l_ref[...] = alpha * l_ref[...] + jnp.sum(pr, axis=-1, keepdims=True)
        acc_ref[...] = alpha * acc_ref[...] + _dot(pr.astype(BF16), v_ref[0])
        m_ref[...] = m_new

    @pl.when(ki < qi)
    def _():
        update(s)

    @pl.when(ki == qi)
    def _():
        row = lax.broadcasted_iota(jnp.int32, (tq, tk), 0)
        col = lax.broadcasted_iota(jnp.int32, (tq, tk), 1)
        update(jnp.where(col <= row, s, -jnp.inf))
        o_ref[...] = (acc_ref[...] / l_ref[...]).astype(o_ref.dtype)


def flash_attention(q, k, v, *, batch, seq, tq=1024):
    tk = tq
    nq = seq // tq
    pairs = [(a, b) for a in range(nq) for b in range(a + 1)]
    qi_tab = jnp.asarray([p[0] for p in pairs], jnp.int32)
    ki_tab = jnp.asarray([p[1] for p in pairs], jnp.int32)
    grid_spec = pltpu.PrefetchScalarGridSpec(
        num_scalar_prefetch=2,
        grid=(batch, MLA_HEADS, len(pairs)),
        in_specs=[pl.BlockSpec((1, tq, MLA_QK), lambda b, h, p, qt, kt: (h, b * nq + qt[p], 0)),
                  pl.BlockSpec((1, tk, MLA_QK), lambda b, h, p, qt, kt: (h, b * nq + kt[p], 0)),
                  pl.BlockSpec((1, tk, MLA_V), lambda b, h, p, qt, kt: (h, b * nq + kt[p], 0))],
        out_specs=pl.BlockSpec((tq, MLA_V), lambda b, h, p, qt, kt: (b * nq + qt[p], h)),
        scratch_shapes=[pltpu.VMEM((tq, 1), F32), pltpu.VMEM((tq, 1), F32), pltpu.VMEM((tq, MLA_V), F32)],
    )
    return pl.pallas_call(
        functools.partial(_flash_kernel, tq=tq, tk=tk),
        grid_spec=grid_spec,
        out_shape=jax.ShapeDtypeStruct((batch * seq, MLA_HEADS * MLA_V), BF16),
        compiler_params=_cparams(("arbitrary", "arbitrary", "arbitrary")),
        name="flash_attention",
    )(qi_tab, ki_tab, q, k, v)


def _merge_kernel(ha_ref, hb_ref, hc_ref, hd_ref, ga_ref, gb_ref, gc_ref, gd_ref, w_ref, o_ref):
    acc = None
    for b, (h_ref, g_ref) in enumerate(((ha_ref, ga_ref), (hb_ref, gb_ref), (hc_ref, gc_ref), (hd_ref, gd_ref))):
        term = g_ref[...].astype(F32) * _dot(h_ref[...], w_ref[b])
        acc = term if acc is None else acc + term
    o_ref[...] = acc.astype(o_ref.dtype)


def merge_branches(h_a, h_b, h_c, h_d, proj_a, w_branch, tm=512, tn=512):
    t = h_a.shape[0]
    width = h_a.shape[1]
    g0 = PA_GATE // tn
    per = D_MODEL // tn
    hspec = pl.BlockSpec((tm, width), lambda i, j: (i, 0))
    gspecs = [pl.BlockSpec((tm, tn), functools.partial(lambda i, j, b: (i, g0 + b * per + j), b=b))
              for b in range(N_BRANCH)]
    return pl.pallas_call(
        _merge_kernel,
        grid=(t // tm, D_MODEL // tn),
        in_specs=[hspec] * 4 + gspecs + [pl.BlockSpec((N_BRANCH, width, tn), lambda i, j: (0, 0, j))],
        out_specs=pl.BlockSpec((tm, tn), lambda i, j: (i, j)),
        out_shape=jax.ShapeDtypeStruct((t, D_MODEL), BF16),
        compiler_params=_cparams(("arbitrary", "arbitrary")),
        name="merge_branches",
    )(h_a, h_b, h_c, h_d, proj_a, proj_a, proj_a, proj_a, w_branch)


def _ffn_up_kernel(x_ref, wg_ref, wu_ref, cwg_ref, cwu_ref, bg_ref, bu_ref, o_ref, gwin_ref, uwin_ref, *, tm):
    s = pl.program_id(2)

    @pl.when(s == 0)
    def _():
        gwin_ref[0:HALO, :] = jnp.zeros((HALO, gwin_ref.shape[1]), F32)
        uwin_ref[0:HALO, :] = jnp.zeros((HALO, uwin_ref.shape[1]), F32)

    x = x_ref[...]
    gwin_ref[HALO:HALO + tm, :] = _dot(x, wg_ref[...])
    uwin_ref[HALO:HALO + tm, :] = _dot(x, wu_ref[...])
    off = HALO - (FFN_CONV - 1)

    def conv(win_ref, cw_ref, b_ref):
        acc = b_ref[...] + win_ref[off:off + tm, :] * cw_ref[0:1, :]
        for j in range(1, FFN_CONV):
            acc = acc + win_ref[off + j:off + j + tm, :] * cw_ref[j:j + 1, :]
        return acc

    gate = conv(gwin_ref, cwg_ref, bg_ref)
    up = conv(uwin_ref, cwu_ref, bu_ref)
    o_ref[...] = (_silu(gate) * up).astype(o_ref.dtype)
    gwin_ref[0:HALO, :] = gwin_ref[tm:tm + HALO, :]
    uwin_ref[0:HALO, :] = uwin_ref[tm:tm + HALO, :]


def ffn_up(hn, w_up, conv_w, conv_b, *, batch, seq, tm=512, tn=512):
    ns = seq // tm
    wg = w_up[:, :FFN_DIM].astype(BF16)
    wu = w_up[:, FFN_DIM:].astype(BF16)
    padw = lambda w: jnp.zeros((8, FFN_DIM), F32).at[:FFN_CONV].set(w.astype(F32))
    row = lambda v: v.reshape(1, FFN_DIM).astype(F32)
    wspec = pl.BlockSpec((D_MODEL, tn), lambda j, b, s: (0, j))
    cspec = pl.BlockSpec((8, tn), lambda j, b, s: (0, j))
    bspec = pl.BlockSpec((1, tn), lambda j, b, s: (0, j))
    return pl.pallas_call(
        functools.partial(_ffn_up_kernel, tm=tm),
        grid=(FFN_DIM // tn, batch, ns),
        in_specs=[pl.BlockSpec((tm, D_MODEL), lambda j, b, s: (b * ns + s, 0)),
                  wspec, wspec, cspec, cspec, bspec, bspec],
        out_specs=pl.BlockSpec((tm, tn), lambda j, b, s: (b * ns + s, j)),
        out_shape=jax.ShapeDtypeStruct((batch * seq, FFN_DIM), BF16),
        scratch_shapes=[pltpu.VMEM((tm + HALO, tn), F32), pltpu.VMEM((tm + HALO, tn), F32)],
        compiler_params=_cparams(("arbitrary", "arbitrary", "arbitrary")),
        name="ffn_up",
    )(hn, wg, wu, padw(conv_w[:, :FFN_DIM]), padw(conv_w[:, FFN_DIM:]),
      row(conv_b[:FFN_DIM]), row(conv_b[FFN_DIM:]))


def _in_proj_weights(w_in):
    o = np.cumsum([0, 1024, 512, 512, 1024, 1024, 8, 8, 2048, 512, 512, 64, 8192])
    col = lambda i: w_in[:, int(o[i]):int(o[i + 1])]
    pool, q, k, v, z, a, b, conf, cq, ckv, krope, gates = (col(i) for i in range(12))
    w_a = jnp.concatenate([q, k, v, conf, pool, z, cq, ckv, gates], axis=1).astype(BF16)
    half = MLA_ROPE // 2
    zeros = lambda n: jnp.zeros((D_MODEL, n), w_in.dtype)
    w_c = jnp.concatenate([a, b, zeros(112), krope, zeros(64),
                           krope[:, half:], krope[:, :half], zeros(64)], axis=1).astype(BF16)
    return w_a, w_c


def kernel(x, positions, mix_norm, w_in, pool_w, pool_scale, gdn_conv_w, gdn_a_log, gdn_dt_bias, gdn_norm, conf_conv_w, conf_conv_b, conf_ln_g, conf_ln_b, mla_q_norm, mla_w_uq, mla_kv_norm, mla_w_ukv, w_pool_out, w_gdn_out, w_conf_out, w_mla_out, w_out, ffn_norm, ffn_w_up, ffn_conv_w, ffn_conv_b, ffn_w_down, final_norm):
    batch, seq, d = x.shape
    t = batch * seq
    xf = x.reshape(t, d)
    cos_t, sin_t = rope_tables(positions)
    for l in range(DEPTH):
        w_a, w_c = _in_proj_weights(w_in[l])
        xn = rmsnorm(xf, mix_norm[l], BF16)
        proj_a = matmul(xn, w_a, tm=1024, tn=1024, tk=D_MODEL, out_dtype=BF16,
                        sigmoid_from=PA_GATE // 1024, name="in_proj")
        proj_c = matmul(xn, w_c, tm=1024, tn=PC_WIDTH, tk=D_MODEL, out_dtype=F32, name="in_proj_small")
        h_a = pool_mixer(proj_a, pool_w[l], pool_scale[l], batch=batch, seq=seq)
        h_b = gated_deltanet(proj_a, proj_c, gdn_conv_w[l], gdn_a_log[l], gdn_dt_bias[l], gdn_norm[l],
                             batch=batch, seq=seq)
        h_c = conformer_conv(proj_a, conf_conv_w[l], conf_conv_b[l], conf_ln_g[l], conf_ln_b[l],
                             batch=batch, seq=seq)
        q, k, v = mla_prep(proj_a, proj_c, cos_t, sin_t, mla_q_norm[l], mla_kv_norm[l],
                           mla_w_uq[l], mla_w_ukv[l])
        h_d = flash_attention(q, k, v, batch=batch, seq=seq)
        w_branch = jnp.stack([w_pool_out[l], w_gdn_out[l], w_conf_out[l], w_mla_out[l]]).astype(BF16)
        merged = merge_branches(h_a, h_b, h_c, h_d, proj_a, w_branch)
        xf = matmul(merged, w_out[l].astype(BF16), tm=1024, tn=1024, tk=D_MODEL, out_dtype=F32,
                    residual=xf, name="out_proj")
        hn = rmsnorm(xf, ffn_norm[l], BF16)
        mid = ffn_up(hn, ffn_w_up[l], ffn_conv_w[l], ffn_conv_b[l], batch=batch, seq=seq)
        xf = matmul(mid, ffn_w_down[l].astype(BF16), tm=512, tn=1024, tk=FFN_DIM // 2, out_dtype=F32,
                    residual=xf, name="ffn_down")
    return rmsnorm(xf, final_norm, F32).reshape(batch, seq, d)
```

```python
import functools
import math

import numpy as np
import jax
import jax.numpy as jnp
from jax import lax
from jax.experimental import pallas as pl
from jax.experimental.pallas import tpu as pltpu

F32 = jnp.float32
BF16 = jnp.bfloat16

D_MODEL = 2048
DEPTH = 2
POOL_WINDOWS = (2, 4, 8, 16)
POOL_GROUPS = 4
POOL_GROUP_DIM = 256
POOL_WIDTH = 1024
GDN_K_HEADS = 4
GDN_V_HEADS = 8
GDN_HEAD_DIM = 128
GDN_KEY_WIDTH = 512
GDN_VAL_WIDTH = 1024
GDN_CONV_CH = 2048
GDN_CONV = 4
GDN_CHUNK = 64
CONF_WIDTH = 1024
CONF_CONV = 31
MLA_HEADS = 8
MLA_NOPE = 128
MLA_ROPE = 64
MLA_V = 128
MLA_QK = MLA_NOPE + MLA_ROPE
MLA_QKP = 256
MLA_Q_RANK = 512
MLA_KV_RANK = 512
ROPE_THETA = 10000.0
N_BRANCH = 4
FFN_DIM = 5632
FFN_CONV = 3
RMS_EPS = 1e-6
LN_EPS = 1e-5

VMEM_LIMIT_BYTES = 56 * 1024 * 1024
HALO = 8
POOL_HALO = 16
CONF_HALO = 32

PA_QKV, PA_CONF, PA_POOL, PA_Z, PA_CQ, PA_CKV, PA_GATE = 0, 2048, 4096, 5120, 6144, 6656, 7168
PA_WIDTH = PA_GATE + N_BRANCH * D_MODEL
PC_WIDTH = 384


def _cparams(semantics):
    return pltpu.CompilerParams(dimension_semantics=semantics, vmem_limit_bytes=VMEM_LIMIT_BYTES)


def _sigmoid(x):
    return jax.nn.sigmoid(x)


def _silu(x):
    return x * jax.nn.sigmoid(x)


def _dot(a, b):
    return jnp.dot(a, b, preferred_element_type=F32)


def _dot_nt(a, b):
    return lax.dot_general(a, b, (((1,), (1,)), ((), ())), preferred_element_type=F32)


def _rmsnorm_kernel(x_ref, g_ref, o_ref):
    x = x_ref[...]
    ms = jnp.mean(x * x, axis=-1, keepdims=True)
    o_ref[...] = (x * lax.rsqrt(ms + RMS_EPS) * g_ref[...]).astype(o_ref.dtype)


def rmsnorm(x, g, out_dtype, tm=512):
    m, d = x.shape
    return pl.pallas_call(
        _rmsnorm_kernel,
        grid=(m // tm,),
        in_specs=[pl.BlockSpec((tm, d), lambda i: (i, 0)), pl.BlockSpec((1, d), lambda i: (0, 0))],
        out_specs=pl.BlockSpec((tm, d), lambda i: (i, 0)),
        out_shape=jax.ShapeDtypeStruct((m, d), out_dtype),
        compiler_params=_cparams(("arbitrary",)),
        name="rmsnorm",
    )(x, g.reshape(1, d).astype(F32))


def _mm_kernel(*refs, nk, sigmoid_from, has_res):
    if has_res:
        a_ref, w_ref, r_ref, o_ref, acc_ref = refs
    else:
        a_ref, w_ref, o_ref, acc_ref = refs
        r_ref = None
    j = pl.program_id(1)
    k = pl.program_id(2)
    part = _dot(a_ref[...], w_ref[...])

    def finish(acc):
        if r_ref is not None:
            acc = acc + r_ref[...]
        if sigmoid_from is None:
            o_ref[...] = acc.astype(o_ref.dtype)
        else:
            @pl.when(j >= sigmoid_from)
            def _():
                o_ref[...] = _sigmoid(acc).astype(o_ref.dtype)

            @pl.when(j < sigmoid_from)
            def _():
                o_ref[...] = acc.astype(o_ref.dtype)

    if nk == 1:
        finish(part)
    else:
        @pl.when(k == 0)
        def _():
            acc_ref[...] = part

        @pl.when(jnp.logical_and(k > 0, k < nk - 1))
        def _():
            acc_ref[...] += part

        @pl.when(k == nk - 1)
        def _():
            finish(acc_ref[...] + part)


def matmul(a, w, *, tm, tn, tk, out_dtype, sigmoid_from=None, residual=None, name="matmul"):
    m, kdim = a.shape
    n = w.shape[1]
    nk = kdim // tk
    in_specs = [pl.BlockSpec((tm, tk), lambda i, j, k: (i, k)),
                pl.BlockSpec((tk, tn), lambda i, j, k: (k, j))]
    args = [a, w]
    if residual is not None:
        in_specs.append(pl.BlockSpec((tm, tn), lambda i, j, k: (i, j)))
        args.append(residual)
    return pl.pallas_call(
        functools.partial(_mm_kernel, nk=nk, sigmoid_from=sigmoid_from, has_res=residual is not None),
        grid=(m // tm, n // tn, nk),
        in_specs=in_specs,
        out_specs=pl.BlockSpec((tm, tn), lambda i, j, k: (i, j)),
        out_shape=jax.ShapeDtypeStruct((m, n), out_dtype),
        scratch_shapes=[pltpu.VMEM((tm, tn) if nk > 1 else (8, 128), F32)],
        compiler_params=_cparams(("arbitrary", "arbitrary", "arbitrary")),
        name=name,
    )(*args)


def _pool_bands(ts):
    i = np.arange(ts)[:, None]
    m = np.arange(ts + POOL_HALO)[None, :]
    lag = (i + POOL_HALO) - m
    return np.stack([((lag >= 0) & (lag < w)) for w in POOL_WINDOWS]).astype(np.float32)


def _pool_kernel(u_ref, band_ref, pw_ref, scale_ref, o_ref, tail_ref, *, ts):
    s = pl.program_id(1)

    @pl.when(s == 0)
    def _():
        tail_ref[...] = jnp.zeros_like(tail_ref)

    t_abs = s * ts + lax.broadcasted_iota(jnp.int32, (ts, 1), 0)
    for g, win_len in enumerate(POOL_WINDOWS):
        cols = slice(g * POOL_GROUP_DIM, (g + 1) * POOL_GROUP_DIM)
        ug = u_ref[:, cols]
        win = jnp.concatenate([tail_ref[:, cols], ug], axis=0)
        sums = _dot(band_ref[g], win)
        cnt = jnp.minimum(t_abs + 1, win_len).astype(F32)
        diff = sums / cnt - ug.astype(F32)
        y = _dot(diff.astype(BF16), pw_ref[g]) * scale_ref[:, cols]
        o_ref[:, cols] = y.astype(o_ref.dtype)
    tail_ref[...] = u_ref[ts - POOL_HALO:, :]


def pool_mixer(proj_a, pool_w, pool_scale, *, batch, seq, ts=512):
    ns = seq // ts
    bands = jnp.asarray(_pool_bands(ts), BF16)
    cb = PA_POOL // POOL_WIDTH
    return pl.pallas_call(
        functools.partial(_pool_kernel, ts=ts),
        grid=(batch, ns),
        in_specs=[pl.BlockSpec((ts, POOL_WIDTH), lambda b, s: (b * ns + s, cb)),
                  pl.BlockSpec((POOL_GROUPS, ts, ts + POOL_HALO), lambda b, s: (0, 0, 0)),
                  pl.BlockSpec((POOL_GROUPS, POOL_GROUP_DIM, POOL_GROUP_DIM), lambda b, s: (0, 0, 0)),
                  pl.BlockSpec((1, POOL_WIDTH), lambda b, s: (0, 0))],
        out_specs=pl.BlockSpec((ts, POOL_WIDTH), lambda b, s: (b * ns + s, 0)),
        out_shape=jax.ShapeDtypeStruct((batch * seq, POOL_WIDTH), BF16),
        scratch_shapes=[pltpu.VMEM((POOL_HALO, POOL_WIDTH), BF16)],
        compiler_params=_cparams(("arbitrary", "arbitrary")),
        name="pool_mixer",
    )(proj_a, bands, pool_w.astype(BF16), pool_scale.reshape(1, POOL_WIDTH).astype(F32))


def _conf_kernel(u_ref, cw_ref, cb_ref, lg_ref, lb_ref, o_ref, win_ref, *, ts, rc):
    s = pl.program_id(1)

    @pl.when(s == 0)
    def _():
        win_ref[0:CONF_HALO, :] = jnp.zeros((CONF_HALO, CONF_WIDTH), F32)

    a = u_ref[:, :CONF_WIDTH].astype(F32)
    gate = u_ref[:, CONF_WIDTH:].astype(F32)
    win_ref[CONF_HALO:CONF_HALO + ts, :] = a * _sigmoid(gate)
    off = CONF_HALO - (CONF_CONV - 1)
    for r in range(ts // rc):
        base = r * rc
        acc = jnp.zeros((rc, CONF_WIDTH), F32)
        for j in range(CONF_CONV):
            acc = acc + win_ref[base + off + j:base + off + j + rc, :] * cw_ref[j:j + 1, :]
        h = acc + cb_ref[...]
        mu = jnp.mean(h, axis=-1, keepdims=True)
        hc = h - mu
        var = jnp.mean(hc * hc, axis=-1, keepdims=True)
        y = hc * lax.rsqrt(var + LN_EPS) * lg_ref[...] + lb_ref[...]
        o_ref[base:base + rc, :] = _silu(y).astype(o_ref.dtype)
    win_ref[0:CONF_HALO, :] = win_ref[ts:ts + CONF_HALO, :]


def conformer_conv(proj_a, conv_w, conv_b, ln_g, ln_b, *, batch, seq, ts=256, rc=16):
    ns = seq // ts
    cw = jnp.zeros((32, CONF_WIDTH), F32).at[:CONF_CONV].set(conv_w.astype(F32))
    row = lambda v: v.reshape(1, CONF_WIDTH).astype(F32)
    cb = PA_CONF // (2 * CONF_WIDTH)
    const = lambda b, s: (0, 0)
    return pl.pallas_call(
        functools.partial(_conf_kernel, ts=ts, rc=rc),
        grid=(batch, ns),
        in_specs=[pl.BlockSpec((ts, 2 * CONF_WIDTH), lambda b, s: (b * ns + s, cb)),
                  pl.BlockSpec((32, CONF_WIDTH), const),
                  pl.BlockSpec((1, CONF_WIDTH), const),
                  pl.BlockSpec((1, CONF_WIDTH), const),
                  pl.BlockSpec((1, CONF_WIDTH), const)],
        out_specs=pl.BlockSpec((ts, CONF_WIDTH), lambda b, s: (b * ns + s, 0)),
        out_shape=jax.ShapeDtypeStruct((batch * seq, CONF_WIDTH), BF16),
        scratch_shapes=[pltpu.VMEM((ts + CONF_HALO, CONF_WIDTH), F32)],
        compiler_params=_cparams(("arbitrary", "arbitrary")),
        name="conformer_conv",
    )(proj_a, cw, row(conv_b), row(ln_g), row(ln_b))


def _split(a):
    hi = a.astype(BF16)
    return hi, (a - hi.astype(F32)).astype(BF16)


def _split_dots(lhs_list, rhs):
    r_hi, r_lo = _split(rhs)
    parts = [_split(l) for l in lhs_list]
    his = [p[0] for p in parts]
    los = [p[1] for p in parts]
    n = len(lhs_list)
    m = lhs_list[0].shape[0]
    top = _dot(jnp.concatenate(his + los, axis=0), r_hi)
    bot = _dot(jnp.concatenate(his, axis=0) if n > 1 else his[0], r_lo)
    return [top[i * m:(i + 1) * m] + (top[(n + i) * m:(n + i + 1) * m] + bot[i * m:(i + 1) * m])
            for i in range(n)]


def _gdn_kernel(qkv_ref, z_ref, ab_ref, cw_ref, alog_ref, dtb_ref, ng_ref, o_ref, win_ref, state_ref):
    c = GDN_CHUNK
    dh = GDN_HEAD_DIM
    step = pl.program_id(1)

    @pl.when(step == 0)
    def _():
        win_ref[0:HALO, :] = jnp.zeros((HALO, GDN_CONV_CH), F32)
        state_ref[...] = jnp.zeros_like(state_ref)

    win_ref[HALO:HALO + c, :] = qkv_ref[...].astype(F32)
    off = HALO - (GDN_CONV - 1)
    y = jnp.zeros((c, GDN_CONV_CH), F32)
    for j in range(GDN_CONV):
        y = y + win_ref[off + j:off + j + c, :] * cw_ref[j:j + 1, :]
    y = _silu(y)
    win_ref[0:HALO, :] = win_ref[c:c + HALO, :]

    ab = ab_ref[...]
    g_all = -jnp.exp(alog_ref[...]) * jnp.logaddexp(ab + dtb_ref[...], 0.0)
    ii = lax.broadcasted_iota(jnp.int32, (c, c), 0)
    jj = lax.broadcasted_iota(jnp.int32, (c, c), 1)
    lower = ii >= jj
    strict = ii > jj
    tri = lower.astype(F32)
    eye = (ii == jj).astype(F32)
    gc_all = jnp.dot(tri, g_all, precision=lax.Precision.HIGHEST, preferred_element_type=F32)
    gc_t = gc_all.T
    beta_all = _sigmoid(ab)

    def l2n(t):
        return t * lax.rsqrt(jnp.sum(t * t, axis=-1, keepdims=True) + 1e-6)

    q_heads, k_heads, kt_heads = [], [], []
    for kh in range(GDN_K_HEADS):
        qh = l2n(y[:, kh * dh:(kh + 1) * dh]) * (dh ** -0.5)
        kh_ = l2n(y[:, GDN_KEY_WIDTH + kh * dh:GDN_KEY_WIDTH + (kh + 1) * dh])
        q_heads.append(qh)
        k_heads.append(kh_)
        kt_heads.append(kh_.T.astype(BF16))

    heads = range(GDN_V_HEADS)
    rep = GDN_V_HEADS // GDN_K_HEADS
    gcol = [gc_all[:, h:h + 1] for h in heads]
    beta = [beta_all[:, 8 + h:9 + h] for h in heads]
    decay = [jnp.where(lower, jnp.exp(jnp.where(lower, gcol[h] - gc_t[h:h + 1, :], 0.0)), 0.0) for h in heads]
    kb = [k_heads[h // rep] * beta[h] for h in heads]
    vb = [y[:, 2 * GDN_KEY_WIDTH + h * dh:2 * GDN_KEY_WIDTH + (h + 1) * dh] * beta[h] for h in heads]
    kq = [_dot(jnp.concatenate([kb[kh * rep + r].astype(BF16) for r in range(rep)]
                               + [q_heads[kh].astype(BF16)], axis=0), kt_heads[kh])
          for kh in range(GDN_K_HEADS)]
    lmat = [jnp.where(strict, kq[h // rep][(h % rep) * c:(h % rep + 1) * c] * decay[h], 0.0) for h in heads]
    qk = [kq[h // rep][rep * c:(rep + 1) * c] * decay[h] for h in heads]
    tinv = [eye - lm for lm in lmat]
    pw = [_split_dots([lm], lm)[0] for lm in lmat]
    for _ in range(4):
        both = [_split_dots([p, t], p) for p, t in zip(pw, tinv)]
        pw = [b[0] for b in both]
        tinv = [t + b[1] for t, b in zip(tinv, both)]
    tinv = [t + _split_dots([t], p)[0] for t, p in zip(tinv, pw)]
    eg = [jnp.exp(g) for g in gcol]
    uw = [_split_dots([tinv[h]], jnp.concatenate([vb[h], kb[h] * eg[h]], axis=1))[0] for h in heads]
    s_old = [state_ref[h] for h in heads]
    ws = [_dot(jnp.concatenate([uw[h][:, dh:].astype(BF16), (q_heads[h // rep] * eg[h]).astype(BF16)], axis=0),
               s_old[h].astype(BF16)) for h in heads]
    v_new = [uw[h][:, :dh] - ws[h][:c] for h in heads]
    o = [ws[h][c:] + _dot(qk[h].astype(BF16), v_new[h].astype(BF16)) for h in heads]
    for h in heads:
        glast = gcol[h][c - 1:c, :]
        vd = (v_new[h] * jnp.exp(glast - gcol[h])).astype(BF16)
        state_ref[h] = s_old[h] * jnp.exp(glast) + _dot(kt_heads[h // rep], vd)
    for h in heads:
        zh = z_ref[:, h * dh:(h + 1) * dh].astype(F32)
        on = o[h] * lax.rsqrt(jnp.mean(o[h] * o[h], axis=-1, keepdims=True) + RMS_EPS) * ng_ref[...]
        o_ref[:, h * dh:(h + 1) * dh] = (on * _silu(zh)).astype(o_ref.dtype)


def gated_deltanet(proj_a, proj_c, conv_w, a_log, dt_bias, norm_g, *, batch, seq):
    c = GDN_CHUNK
    nc = seq // c
    cw = jnp.zeros((8, GDN_CONV_CH), F32).at[:GDN_CONV].set(conv_w.astype(F32))
    pad = lambda v: jnp.zeros((1, 128), F32).at[0, :GDN_V_HEADS].set(v.astype(F32))
    const = lambda b, s: (0, 0)
    return pl.pallas_call(
        _gdn_kernel,
        grid=(batch, nc),
        in_specs=[pl.BlockSpec((c, GDN_CONV_CH), lambda b, s: (b * nc + s, PA_QKV // GDN_CONV_CH)),
                  pl.BlockSpec((c, GDN_VAL_WIDTH), lambda b, s: (b * nc + s, PA_Z // GDN_VAL_WIDTH)),
                  pl.BlockSpec((c, 128), lambda b, s: (b * nc + s, 0)),
                  pl.BlockSpec((8, GDN_CONV_CH), const),
                  pl.BlockSpec((1, 128), const),
                  pl.BlockSpec((1, 128), const),
                  pl.BlockSpec((1, GDN_HEAD_DIM), const)],
        out_specs=pl.BlockSpec((c, GDN_VAL_WIDTH), lambda b, s: (b * nc + s, 0)),
        out_shape=jax.ShapeDtypeStruct((batch * seq, GDN_VAL_WIDTH), BF16),
        scratch_shapes=[pltpu.VMEM((c + HALO, GDN_CONV_CH), F32),
                        pltpu.VMEM((GDN_V_HEADS, GDN_HEAD_DIM, GDN_HEAD_DIM), F32)],
        compiler_params=_cparams(("arbitrary", "arbitrary")),
        name="gated_deltanet",
    )(proj_a, proj_a, proj_c, cw, pad(a_log), pad(dt_bias), norm_g.reshape(1, GDN_HEAD_DIM).astype(F32))


def _rope_kernel(pos_ref, freq_ref, sign_ref, c_ref, s_ref):
    ang = pos_ref[...].astype(F32) * freq_ref[...]
    c_ref[...] = jnp.cos(ang)
    s_ref[...] = jnp.sin(ang) * sign_ref[...]


def rope_tables(positions, tm=512):
    t = positions.size
    inv_freq = ROPE_THETA ** (-jnp.arange(0, MLA_ROPE, 2, dtype=F32) / MLA_ROPE)
    pad = jnp.zeros((MLA_QKP - MLA_QK,), F32)
    freq = jnp.concatenate([jnp.zeros((MLA_NOPE,), F32), inv_freq, inv_freq, pad]).reshape(1, MLA_QKP)
    half = MLA_ROPE // 2
    sign = jnp.concatenate([jnp.zeros((MLA_NOPE,), F32), -jnp.ones((half,), F32),
                            jnp.ones((half,), F32), pad]).reshape(1, MLA_QKP)
    return pl.pallas_call(
        _rope_kernel,
        grid=(t // tm,),
        in_specs=[pl.BlockSpec((tm, 1), lambda i: (i, 0)),
                  pl.BlockSpec((1, MLA_QKP), lambda i: (0, 0)),
                  pl.BlockSpec((1, MLA_QKP), lambda i: (0, 0))],
        out_specs=[pl.BlockSpec((tm, MLA_QKP), lambda i: (i, 0))] * 2,
        out_shape=[jax.ShapeDtypeStruct((t, MLA_QKP), F32)] * 2,
        compiler_params=_cparams(("arbitrary",)),
        name="rope_tables",
    )(positions.reshape(t, 1).astype(jnp.int32), freq, sign)


def _mla_prep_kernel(cq_ref, ckv_ref, kr_ref, krs_ref, cos_ref, sin_ref, qn_ref, kvn_ref,
                     wq_ref, wqs_ref, wkv_ref, q_out, k_out, v_out):
    def rms(x, g):
        return x * lax.rsqrt(jnp.mean(x * x, axis=-1, keepdims=True) + RMS_EPS) * g

    cqn = rms(cq_ref[...].astype(F32), qn_ref[...]).astype(BF16)
    ckvn = rms(ckv_ref[...].astype(F32), kvn_ref[...]).astype(BF16)
    cos = cos_ref[...]
    sin = sin_ref[...]
    kpe = (kr_ref[...] * cos[:, MLA_NOPE:] + krs_ref[...] * sin[:, MLA_NOPE:]).astype(k_out.dtype)
    q_all = _dot(cqn, wq_ref[...])
    qs_all = _dot(cqn, wqs_ref[...])
    kv_all = _dot(ckvn, wkv_ref[...])
    scale = MLA_QK ** -0.5
    for h in range(MLA_HEADS):
        cols = slice(h * MLA_QKP, (h + 1) * MLA_QKP)
        q_out[h] = ((q_all[:, cols] * cos + qs_all[:, cols] * sin) * scale).astype(q_out.dtype)
        base = h * (MLA_NOPE + MLA_V)
        k_out[h, :, :MLA_NOPE] = kv_all[:, base:base + MLA_NOPE].astype(k_out.dtype)
        k_out[h, :, MLA_NOPE:] = kpe
        v_out[h] = kv_all[:, base + MLA_NOPE:base + MLA_NOPE + MLA_V].astype(v_out.dtype)


def mla_prep(proj_a, proj_c, cos_t, sin_t, q_norm, kv_norm, w_uq, w_ukv, tm=512):
    t = proj_a.shape[0]
    hh = MLA_HEADS
    wq = w_uq.reshape(MLA_Q_RANK, hh, MLA_QK)
    half = MLA_ROPE // 2
    zeros = lambda n: jnp.zeros((MLA_Q_RANK, hh, n), w_uq.dtype)
    padw = lambda w: w.reshape(MLA_Q_RANK, hh * MLA_QKP).astype(BF16)
    wq_pad = padw(jnp.concatenate([wq, zeros(MLA_QKP - MLA_QK)], axis=-1))
    wq_swap = padw(jnp.concatenate([zeros(MLA_NOPE), wq[:, :, MLA_NOPE + half:],
                                    wq[:, :, MLA_NOPE:MLA_NOPE + half], zeros(MLA_QKP - MLA_QK)], axis=-1))
    row = lambda v: v.reshape(1, -1).astype(F32)
    const = lambda i: (0, 0)
    heads_tile = lambda i: (0, i, 0)
    return pl.pallas_call(
        _mla_prep_kernel,
        grid=(t // tm,),
        in_specs=[pl.BlockSpec((tm, MLA_Q_RANK), lambda i: (i, PA_CQ // MLA_Q_RANK)),
                  pl.BlockSpec((tm, MLA_KV_RANK), lambda i: (i, PA_CKV // MLA_KV_RANK)),
                  pl.BlockSpec((tm, 128), lambda i: (i, 1)),
                  pl.BlockSpec((tm, 128), lambda i: (i, 2)),
                  pl.BlockSpec((tm, MLA_QKP), lambda i: (i, 0)),
                  pl.BlockSpec((tm, MLA_QKP), lambda i: (i, 0)),
                  pl.BlockSpec((1, MLA_Q_RANK), const),
                  pl.BlockSpec((1, MLA_KV_RANK), const),
                  pl.BlockSpec((MLA_Q_RANK, hh * MLA_QKP), const),
                  pl.BlockSpec((MLA_Q_RANK, hh * MLA_QKP), const),
                  pl.BlockSpec((MLA_KV_RANK, hh * (MLA_NOPE + MLA_V)), const)],
        out_specs=[pl.BlockSpec((hh, tm, MLA_QKP), heads_tile),
                   pl.BlockSpec((hh, tm, MLA_QKP), heads_tile),
                   pl.BlockSpec((hh, tm, MLA_V), heads_tile)],
        out_shape=[jax.ShapeDtypeStruct((hh, t, MLA_QKP), BF16),
                   jax.ShapeDtypeStruct((hh, t, MLA_QKP), BF16),
                   jax.ShapeDtypeStruct((hh, t, MLA_V), BF16)],
        compiler_params=_cparams(("arbitrary",)),
        name="mla_prep",
    )(proj_a, proj_a, proj_c, proj_c, cos_t, sin_t, row(q_norm), row(kv_norm),
      wq_pad, wq_swap, w_ukv.astype(BF16))


def _flash_kernel(qi_tab, ki_tab, q_ref, k_ref, v_ref, o_ref,
                  s_ref, p_ref, mpart_ref, m_ref, alpha_ref, l_ref, acc_ref, *, tq, tk, rg):
    p = pl.program_id(2)
    qi = qi_tab[p]
    ki = ki_tab[p]
    nl = tk // 128

    @pl.when(ki == 0)
    def _():
        m_ref[...] = jnp.full_like(m_ref, -jnp.inf)
        l_ref[...] = jnp.zeros_like(l_ref)
        acc_ref[...] = jnp.zeros_like(acc_ref)

    s_ref[...] = _dot_nt(q_ref[0], k_ref[0])

    def tile_update(masked):
        def block(r, j):
            rows = pl.ds(pl.multiple_of(r * rg, rg), rg)
            blk = s_ref[rows, j * 128:(j + 1) * 128]
            if masked:
                row_id = r * rg + lax.broadcasted_iota(jnp.int32, (rg, 128), 0)
                col_id = j * 128 + lax.broadcasted_iota(jnp.int32, (rg, 128), 1)
                blk = jnp.where(col_id <= row_id, blk, -jnp.inf)
            return rows, blk

        def max_rows(r, carry):
            part = None
            for j in range(nl):
                rows, blk = block(r, j)
                part = blk if part is None else jnp.maximum(part, blk)
            mpart_ref[rows, :] = part
            return carry

        lax.fori_loop(0, tq // rg, max_rows, 0)
        m_old = m_ref[...]
        m_new = jnp.maximum(m_old, jnp.max(mpart_ref[...], axis=-1, keepdims=True))
        alpha_ref[...] = jnp.exp(m_old - m_new)
        m_ref[...] = m_new

        def exp_rows(r, carry):
            part = None
            for j in range(nl):
                rows, blk = block(r, j)
                pj = jnp.exp(blk - m_ref[rows, :])
                p_ref[rows, j * 128:(j + 1) * 128] = pj.astype(p_ref.dtype)
                part = pj if part is None else part + pj
            l_ref[rows, :] = alpha_ref[rows, :] * l_ref[rows, :] + part
            return carry

        lax.fori_loop(0, tq // rg, exp_rows, 0)
        acc_ref[...] = alpha_ref[...] * acc_ref[...] + _dot(p_ref[...], v_ref[0])

    @pl.when(ki < qi)
    def _():
        tile_update(False)

    @pl.when(ki == qi)
    def _():
        tile_update(True)
        denom = jnp.sum(l_ref[...], axis=-1, keepdims=True)
        o_ref[...] = (acc_ref[...] / denom).astype(o_ref.dtype)


def flash_attention(q, k, v, *, batch, seq, tq=1024, rg=32):
    tk = tq
    nq = seq // tq
    pairs = [(a, b) for a in range(nq) for b in range(a + 1)]
    qi_tab = jnp.asarray([p[0] for p in pairs], jnp.int32)
    ki_tab = jnp.asarray([p[1] for p in pairs], jnp.int32)
    lanes = pltpu.VMEM((tq, 128), F32)
    grid_spec = pltpu.PrefetchScalarGridSpec(
        num_scalar_prefetch=2,
        grid=(batch, MLA_HEADS, len(pairs)),
        in_specs=[pl.BlockSpec((1, tq, MLA_QKP), lambda b, h, p, qt, kt: (h, b * nq + qt[p], 0)),
                  pl.BlockSpec((1, tk, MLA_QKP), lambda b, h, p, qt, kt: (h, b * nq + kt[p], 0)),
                  pl.BlockSpec((1, tk, MLA_V), lambda b, h, p, qt, kt: (h, b * nq + kt[p], 0))],
        out_specs=pl.BlockSpec((tq, MLA_V), lambda b, h, p, qt, kt: (b * nq + qt[p], h)),
        scratch_shapes=[pltpu.VMEM((tq, tk), F32), pltpu.VMEM((tq, tk), BF16),
                        lanes, lanes, lanes, lanes, pltpu.VMEM((tq, MLA_V), F32)],
    )
    return pl.pallas_call(
        functools.partial(_flash_kernel, tq=tq, tk=tk, rg=rg),
        grid_spec=grid_spec,
        out_shape=jax.ShapeDtypeStruct((batch * seq, MLA_HEADS * MLA_V), BF16),
        compiler_params=_cparams(("arbitrary", "arbitrary", "arbitrary")),
        name="flash_attention",
    )(qi_tab, ki_tab, q, k, v)


def _merge_kernel(ha_ref, hb_ref, hc_ref, hd_ref, ga_ref, gb_ref, gc_ref, gd_ref, w_ref, o_ref):
    acc = None
    for b, (h_ref, g_ref) in enumerate(((ha_ref, ga_ref), (hb_ref, gb_ref), (hc_ref, gc_ref), (hd_ref, gd_ref))):
        term = g_ref[...].astype(F32) * _dot(h_ref[...], w_ref[b])
        acc = term if acc is None else acc + term
    o_ref[...] = acc.astype(o_ref.dtype)


def merge_branches(h_a, h_b, h_c, h_d, proj_a, w_branch, tm=512, tn=512):
    t = h_a.shape[0]
    width = h_a.shape[1]
    g0 = PA_GATE // tn
    per = D_MODEL // tn
    hspec = pl.BlockSpec((tm, width), lambda i, j: (i, 0))
    gspecs = [pl.BlockSpec((tm, tn), functools.partial(lambda i, j, b: (i, g0 + b * per + j), b=b))
              for b in range(N_BRANCH)]
    return pl.pallas_call(
        _merge_kernel,
        grid=(t // tm, D_MODEL // tn),
        in_specs=[hspec] * 4 + gspecs + [pl.BlockSpec((N_BRANCH, width, tn), lambda i, j: (0, 0, j))],
        out_specs=pl.BlockSpec((tm, tn), lambda i, j: (i, j)),
        out_shape=jax.ShapeDtypeStruct((t, D_MODEL), BF16),
        compiler_params=_cparams(("arbitrary", "arbitrary")),
        name="merge_branches",
    )(h_a, h_b, h_c, h_d, proj_a, proj_a, proj_a, proj_a, w_branch)


def _ffn_up_kernel(x_ref, wg_ref, wu_ref, cwg_ref, cwu_ref, bg_ref, bu_ref, o_ref, gwin_ref, uwin_ref, *, tm):
    s = pl.program_id(2)

    @pl.when(s == 0)
    def _():
        gwin_ref[0:HALO, :] = jnp.zeros((HALO, gwin_ref.shape[1]), F32)
        uwin_ref[0:HALO, :] = jnp.zeros((HALO, uwin_ref.shape[1]), F32)

    x = x_ref[...]
    gwin_ref[HALO:HALO + tm, :] = _dot(x, wg_ref[...])
    uwin_ref[HALO:HALO + tm, :] = _dot(x, wu_ref[...])
    off = HALO - (FFN_CONV - 1)

    def conv(win_ref, cw_ref, b_ref):
        acc = b_ref[...] + win_ref[off:off + tm, :] * cw_ref[0:1, :]
        for j in range(1, FFN_CONV):
            acc = acc + win_ref[off + j:off + j + tm, :] * cw_ref[j:j + 1, :]
        return acc

    gate = conv(gwin_ref, cwg_ref, bg_ref)
    up = conv(uwin_ref, cwu_ref, bu_ref)
    o_ref[...] = (_silu(gate) * up).astype(o_ref.dtype)
    gwin_ref[0:HALO, :] = gwin_ref[tm:tm + HALO, :]
    uwin_ref[0:HALO, :] = uwin_ref[tm:tm + HALO, :]


def ffn_up(hn, w_up, conv_w, conv_b, *, batch, seq, tm=512, tn=512):
    ns = seq // tm
    wg = w_up[:, :FFN_DIM].astype(BF16)
    wu = w_up[:, FFN_DIM:].astype(BF16)
    padw = lambda w: jnp.zeros((8, FFN_DIM), F32).at[:FFN_CONV].set(w.astype(F32))
    row = lambda v: v.reshape(1, FFN_DIM).astype(F32)
    wspec = pl.BlockSpec((D_MODEL, tn), lambda j, b, s: (0, j))
    cspec = pl.BlockSpec((8, tn), lambda j, b, s: (0, j))
    bspec = pl.BlockSpec((1, tn), lambda j, b, s: (0, j))
    return pl.pallas_call(
        functools.partial(_ffn_up_kernel, tm=tm),
        grid=(FFN_DIM // tn, batch, ns),
        in_specs=[pl.BlockSpec((tm, D_MODEL), lambda j, b, s: (b * ns + s, 0)),
                  wspec, wspec, cspec, cspec, bspec, bspec],
        out_specs=pl.BlockSpec((tm, tn), lambda j, b, s: (b * ns + s, j)),
        out_shape=jax.ShapeDtypeStruct((batch * seq, FFN_DIM), BF16),
        scratch_shapes=[pltpu.VMEM((tm + HALO, tn), F32), pltpu.VMEM((tm + HALO, tn), F32)],
        compiler_params=_cparams(("arbitrary", "arbitrary", "arbitrary")),
        name="ffn_up",
    )(hn, wg, wu, padw(conv_w[:, :FFN_DIM]), padw(conv_w[:, FFN_DIM:]),
      row(conv_b[:FFN_DIM]), row(conv_b[FFN_DIM:]))


def _in_proj_weights(w_in):
    o = np.cumsum([0, 1024, 512, 512, 1024, 1024, 8, 8, 2048, 512, 512, 64, 8192])
    col = lambda i: w_in[:, int(o[i]):int(o[i + 1])]
    pool, q, k, v, z, a, b, conf, cq, ckv, krope, gates = (col(i) for i in range(12))
    w_a = jnp.concatenate([q, k, v, conf, pool, z, cq, ckv, gates], axis=1).astype(BF16)
    half = MLA_ROPE // 2
    zeros = lambda n: jnp.zeros((D_MODEL, n), w_in.dtype)
    w_c = jnp.concatenate([a, b, zeros(112), krope, zeros(64),
                           krope[:, half:], krope[:, :half], zeros(64)], axis=1).astype(BF16)
    return w_a, w_c


def kernel(x, positions, mix_norm, w_in, pool_w, pool_scale, gdn_conv_w, gdn_a_log, gdn_dt_bias, gdn_norm, conf_conv_w, conf_conv_b, conf_ln_g, conf_ln_b, mla_q_norm, mla_w_uq, mla_kv_norm, mla_w_ukv, w_pool_out, w_gdn_out, w_conf_out, w_mla_out, w_out, ffn_norm, ffn_w_up, ffn_conv_w, ffn_conv_b, ffn_w_down, final_norm):
    batch, seq, d = x.shape
    t = batch * seq
    xf = x.reshape(t, d)
    cos_t, sin_t = rope_tables(positions)
    for l in range(DEPTH):
        w_a, w_c = _in_proj_weights(w_in[l])
        xn = rmsnorm(xf, mix_norm[l], BF16)
        proj_a = matmul(xn, w_a, tm=1024, tn=1024, tk=D_MODEL, out_dtype=BF16,
                        sigmoid_from=PA_GATE // 1024, name="in_proj")
        proj_c = matmul(xn, w_c, tm=1024, tn=PC_WIDTH, tk=D_MODEL, out_dtype=F32, name="in_proj_small")
        h_a = pool_mixer(proj_a, pool_w[l], pool_scale[l], batch=batch, seq=seq)
        h_b = gated_deltanet(proj_a, proj_c, gdn_conv_w[l], gdn_a_log[l], gdn_dt_bias[l], gdn_norm[l],
                             batch=batch, seq=seq)
        h_c = conformer_conv(proj_a, conf_conv_w[l], conf_conv_b[l], conf_ln_g[l], conf_ln_b[l],
                             batch=batch, seq=seq)
        q, k, v = mla_prep(proj_a, proj_c, cos_t, sin_t, mla_q_norm[l], mla_kv_norm[l],
                           mla_w_uq[l], mla_w_ukv[l])
        h_d = flash_attention(q, k, v, batch=batch, seq=seq)
        w_branch = jnp.stack([w_pool_out[l], w_gdn_out[l], w_conf_out[l], w_mla_out[l]]).astype(BF16)
        merged = merge_branches(h_a, h_b, h_c, h_d, proj_a, w_branch)
        xf = matmul(merged, w_out[l].astype(BF16), tm=1024, tn=1024, tk=D_MODEL, out_dtype=F32,
                    residual=xf, name="out_proj")
        hn = rmsnorm(xf, ffn_norm[l], BF16)
        mid = ffn_up(hn, ffn_w_up[l], ffn_conv_w[l], ffn_conv_b[l], batch=batch, seq=seq)
        xf = matmul(mid, ffn_w_down[l].astype(BF16), tm=512, tn=1024, tk=FFN_DIM // 2, out_dtype=F32,
                    residual=xf, name="ffn_down")
    return rmsnorm(xf, final_norm, F32).reshape(batch, seq, d)
```

```python
import functools
import math

import numpy as np
import jax
import jax.numpy as jnp
from jax import lax
from jax.experimental import pallas as pl
from jax.experimental.pallas import tpu as pltpu

F32 = jnp.float32
BF16 = jnp.bfloat16

D_MODEL = 2048
DEPTH = 2
POOL_WINDOWS = (2, 4, 8, 16)
POOL_GROUPS = 4
POOL_GROUP_DIM = 256
POOL_WIDTH = 1024
GDN_K_HEADS = 4
GDN_V_HEADS = 8
GDN_HEAD_DIM = 128
GDN_KEY_WIDTH = 512
GDN_VAL_WIDTH = 1024
GDN_CONV_CH = 2048
GDN_CONV = 4
GDN_CHUNK = 64
CONF_WIDTH = 1024
CONF_CONV = 31
MLA_HEADS = 8
MLA_NOPE = 128
MLA_ROPE = 64
MLA_V = 128
MLA_QK = MLA_NOPE + MLA_ROPE
MLA_QKP = 256
MLA_Q_RANK = 512
MLA_KV_RANK = 512
ROPE_THETA = 10000.0
N_BRANCH = 4
FFN_DIM = 5632
FFN_CONV = 3
RMS_EPS = 1e-6
LN_EPS = 1e-5

VMEM_LIMIT_BYTES = 56 * 1024 * 1024
LANES = 128
HALO = 8
POOL_HALO = 16
CONF_HALO = 32

PA_QKV, PA_CONF, PA_POOL, PA_Z, PA_CQ, PA_CKV, PA_GATE = 0, 2048, 4096, 5120, 6144, 6656, 7168
PA_WIDTH = PA_GATE + N_BRANCH * D_MODEL
PC_WIDTH = 384


def _cparams(semantics):
    return pltpu.CompilerParams(dimension_semantics=semantics, vmem_limit_bytes=VMEM_LIMIT_BYTES)


def _sigmoid(x):
    return jax.nn.sigmoid(x)


def _silu(x):
    return x * jax.nn.sigmoid(x)


def _dot(a, b):
    return jnp.dot(a, b, preferred_element_type=F32)


def _dot_nt(a, b):
    return lax.dot_general(a, b, (((1,), (1,)), ((), ())), preferred_element_type=F32)


def _rmsnorm_kernel(x_ref, g_ref, o_ref):
    x = x_ref[...]
    ms = jnp.mean(x * x, axis=-1, keepdims=True)
    o_ref[...] = (x * lax.rsqrt(ms + RMS_EPS) * g_ref[...]).astype(o_ref.dtype)


def rmsnorm(x, g, out_dtype, tm=512):
    m, d = x.shape
    return pl.pallas_call(
        _rmsnorm_kernel,
        grid=(m // tm,),
        in_specs=[pl.BlockSpec((tm, d), lambda i: (i, 0)), pl.BlockSpec((1, d), lambda i: (0, 0))],
        out_specs=pl.BlockSpec((tm, d), lambda i: (i, 0)),
        out_shape=jax.ShapeDtypeStruct((m, d), out_dtype),
        compiler_params=_cparams(("arbitrary",)),
        name="rmsnorm",
    )(x, g.reshape(1, d).astype(F32))


def _mm_kernel(*refs, nk, sigmoid_from, has_res, norm):
    refs = list(refs)
    a_ref, w_ref = refs[:2]
    pos = 2
    r_ref = g_ref = None
    if has_res:
        r_ref = refs[pos]
        pos += 1
    if norm is not None:
        g_ref = refs[pos]
        pos += 1
    outs = refs[pos:-1]
    acc_ref = refs[-1]
    j = pl.program_id(1)
    k = pl.program_id(2)
    part = _dot(a_ref[...], w_ref[...])

    def finish(acc):
        if r_ref is not None:
            acc = acc + r_ref[...]
        if norm is not None:
            y = acc * lax.rsqrt(jnp.mean(acc * acc, axis=-1, keepdims=True) + RMS_EPS) * g_ref[...]
            if norm == "also":
                outs[0][...] = acc.astype(outs[0].dtype)
                outs[1][...] = y.astype(outs[1].dtype)
            else:
                outs[0][...] = y.astype(outs[0].dtype)
        elif sigmoid_from is None:
            outs[0][...] = acc.astype(outs[0].dtype)
        else:
            @pl.when(j >= sigmoid_from)
            def _():
                outs[0][...] = _sigmoid(acc).astype(outs[0].dtype)

            @pl.when(j < sigmoid_from)
            def _():
                outs[0][...] = acc.astype(outs[0].dtype)

    if nk == 1:
        finish(part)
    else:
        @pl.when(k == 0)
        def _():
            acc_ref[...] = part

        @pl.when(jnp.logical_and(k > 0, k < nk - 1))
        def _():
            acc_ref[...] += part

        @pl.when(k == nk - 1)
        def _():
            finish(acc_ref[...] + part)


def matmul(a, w, *, tm, tn, tk, out_dtype, sigmoid_from=None, residual=None, norm=None, norm_gain=None,
           norm_dtype=None, name="matmul"):
    m, kdim = a.shape
    n = w.shape[1]
    nk = kdim // tk
    in_specs = [pl.BlockSpec((tm, tk), lambda i, j, k: (i, k)),
                pl.BlockSpec((tk, tn), lambda i, j, k: (k, j))]
    args = [a, w]
    if residual is not None:
        in_specs.append(pl.BlockSpec((tm, tn), lambda i, j, k: (i, j)))
        args.append(residual)
    out_spec = pl.BlockSpec((tm, tn), lambda i, j, k: (i, j))
    out_specs, out_shape = out_spec, jax.ShapeDtypeStruct((m, n), out_dtype)
    if norm is not None:
        assert tn == n and norm in ("also", "only")
        in_specs.append(pl.BlockSpec((1, n), lambda i, j, k: (0, 0)))
        args.append(norm_gain.reshape(1, n).astype(F32))
        if norm == "also":
            out_specs = [out_spec, out_spec]
            out_shape = [out_shape, jax.ShapeDtypeStruct((m, n), norm_dtype)]
    return pl.pallas_call(
        functools.partial(_mm_kernel, nk=nk, sigmoid_from=sigmoid_from, has_res=residual is not None, norm=norm),
        grid=(m // tm, n // tn, nk),
        in_specs=in_specs,
        out_specs=out_specs,
        out_shape=out_shape,
        scratch_shapes=[pltpu.VMEM((tm, tn) if nk > 1 else (HALO, LANES), F32)],
        compiler_params=_cparams(("arbitrary", "arbitrary", "arbitrary")),
        name=name,
    )(*args)


def _pool_bands(ts):
    i = np.arange(ts)[:, None]
    m = np.arange(ts + POOL_HALO)[None, :]
    lag = (i + POOL_HALO) - m
    return np.stack([((lag >= 0) & (lag < w)) for w in POOL_WINDOWS]).astype(np.float32)


def _pool_kernel(u_ref, band_ref, pw_ref, scale_ref, o_ref, tail_ref, *, ts):
    s = pl.program_id(1)

    @pl.when(s == 0)
    def _():
        tail_ref[...] = jnp.zeros_like(tail_ref)

    t_abs = s * ts + lax.broadcasted_iota(jnp.int32, (ts, 1), 0)
    for g, win_len in enumerate(POOL_WINDOWS):
        cols = slice(g * POOL_GROUP_DIM, (g + 1) * POOL_GROUP_DIM)
        ug = u_ref[:, cols]
        win = jnp.concatenate([tail_ref[:, cols], ug], axis=0)
        sums = _dot(band_ref[g], win)
        cnt = jnp.minimum(t_abs + 1, win_len).astype(F32)
        diff = sums / cnt - ug.astype(F32)
        y = _dot(diff.astype(BF16), pw_ref[g]) * scale_ref[:, cols]
        o_ref[:, cols] = y.astype(o_ref.dtype)
    tail_ref[...] = u_ref[ts - POOL_HALO:, :]


def pool_mixer(proj_a, pool_w, pool_scale, *, batch, seq, ts=512):
    ns = seq // ts
    bands = jnp.asarray(_pool_bands(ts), BF16)
    cb = PA_POOL // POOL_WIDTH
    return pl.pallas_call(
        functools.partial(_pool_kernel, ts=ts),
        grid=(batch, ns),
        in_specs=[pl.BlockSpec((ts, POOL_WIDTH), lambda b, s: (b * ns + s, cb)),
                  pl.BlockSpec((POOL_GROUPS, ts, ts + POOL_HALO), lambda b, s: (0, 0, 0)),
                  pl.BlockSpec((POOL_GROUPS, POOL_GROUP_DIM, POOL_GROUP_DIM), lambda b, s: (0, 0, 0)),
                  pl.BlockSpec((1, POOL_WIDTH), lambda b, s: (0, 0))],
        out_specs=pl.BlockSpec((ts, POOL_WIDTH), lambda b, s: (b * ns + s, 0)),
        out_shape=jax.ShapeDtypeStruct((batch * seq, POOL_WIDTH), BF16),
        scratch_shapes=[pltpu.VMEM((POOL_HALO, POOL_WIDTH), BF16)],
        compiler_params=_cparams(("arbitrary", "arbitrary")),
        name="pool_mixer",
    )(proj_a, bands, pool_w.astype(BF16), pool_scale.reshape(1, POOL_WIDTH).astype(F32))


def _conf_kernel(u_ref, cw_ref, cb_ref, lg_ref, lb_ref, o_ref, win_ref, sh_ref, y_ref, *, ts, rc, rn):
    s = pl.program_id(1)
    span = ts + CONF_HALO - HALO

    @pl.when(s == 0)
    def _():
        win_ref[0:CONF_HALO, :] = jnp.zeros((CONF_HALO, CONF_WIDTH), F32)

    a = u_ref[:, :CONF_WIDTH].astype(F32)
    gate = u_ref[:, CONF_WIDTH:].astype(F32)
    win_ref[CONF_HALO:CONF_HALO + ts, :] = a * _sigmoid(gate)
    for r in range(1, HALO):
        sh_ref[r - 1, 0:span, :] = win_ref[r:r + span, :]
    off = CONF_HALO - (CONF_CONV - 1)

    for base in range(0, ts, rc):
        for c in range(CONF_WIDTH // LANES):
            lanes = slice(c * LANES, (c + 1) * LANES)
            acc = None
            for j in range(CONF_CONV):
                start = base + ((off + j) // HALO) * HALO
                r = (off + j) % HALO
                x = win_ref[start:start + rc, lanes] if r == 0 else sh_ref[r - 1, start:start + rc, lanes]
                term = x * cw_ref[j:j + 1, lanes]
                acc = term if acc is None else acc + term
            y_ref[base:base + rc, lanes] = acc + cb_ref[:, lanes]

    def norm_rows(i, carry):
        rows = pl.ds(pl.multiple_of(i * rn, rn), rn)
        h = y_ref[rows, :]
        mu = jnp.mean(h, axis=-1, keepdims=True)
        hc = h - mu
        var = jnp.mean(hc * hc, axis=-1, keepdims=True)
        y = hc * lax.rsqrt(var + LN_EPS) * lg_ref[...] + lb_ref[...]
        o_ref[rows, :] = _silu(y).astype(o_ref.dtype)
        return carry

    lax.fori_loop(0, ts // rn, norm_rows, 0, unroll=4)
    win_ref[0:CONF_HALO, :] = win_ref[ts:ts + CONF_HALO, :]


def conformer_conv(proj_a, conv_w, conv_b, ln_g, ln_b, *, batch, seq, ts=256, rc=128, rn=32):
    ns = seq // ts
    cw = jnp.zeros((CONF_HALO, CONF_WIDTH), F32).at[:CONF_CONV].set(conv_w.astype(F32))
    row = lambda v: v.reshape(1, CONF_WIDTH).astype(F32)
    cb = PA_CONF // (2 * CONF_WIDTH)
    const = lambda b, s: (0, 0)
    return pl.pallas_call(
        functools.partial(_conf_kernel, ts=ts, rc=rc, rn=rn),
        grid=(batch, ns),
        in_specs=[pl.BlockSpec((ts, 2 * CONF_WIDTH), lambda b, s: (b * ns + s, cb)),
                  pl.BlockSpec((CONF_HALO, CONF_WIDTH), const),
                  pl.BlockSpec((1, CONF_WIDTH), const),
                  pl.BlockSpec((1, CONF_WIDTH), const),
                  pl.BlockSpec((1, CONF_WIDTH), const)],
        out_specs=pl.BlockSpec((ts, CONF_WIDTH), lambda b, s: (b * ns + s, 0)),
        out_shape=jax.ShapeDtypeStruct((batch * seq, CONF_WIDTH), BF16),
        scratch_shapes=[pltpu.VMEM((ts + CONF_HALO, CONF_WIDTH), F32),
                        pltpu.VMEM((HALO - 1, ts + CONF_HALO - HALO, CONF_WIDTH), F32),
                        pltpu.VMEM((ts, CONF_WIDTH), F32)],
        compiler_params=_cparams(("arbitrary", "arbitrary")),
        name="conformer_conv",
    )(proj_a, cw, row(conv_b), row(ln_g), row(ln_b))


def _split(a):
    hi = a.astype(BF16)
    return hi, (a - hi.astype(F32)).astype(BF16)


def _split_dots(lhs_list, rhs):
    r_hi, r_lo = _split(rhs)
    parts = [_split(l) for l in lhs_list]
    his = [p[0] for p in parts]
    los = [p[1] for p in parts]
    n = len(lhs_list)
    m = lhs_list[0].shape[0]
    top = _dot(jnp.concatenate(his + los, axis=0), r_hi)
    bot = _dot(jnp.concatenate(his, axis=0) if n > 1 else his[0], r_lo)
    return [top[i * m:(i + 1) * m] + (top[(n + i) * m:(n + i + 1) * m] + bot[i * m:(i + 1) * m])
            for i in range(n)]


def _gdn_kernel(qkv_ref, z_ref, ab_ref, cw_ref, alog_ref, dtb_ref, ng_ref, o_ref, win_ref, state_ref):
    c = GDN_CHUNK
    dh = GDN_HEAD_DIM
    step = pl.program_id(1)

    @pl.when(step == 0)
    def _():
        win_ref[0:HALO, :] = jnp.zeros((HALO, GDN_CONV_CH), F32)
        state_ref[...] = jnp.zeros_like(state_ref)

    win_ref[HALO:HALO + c, :] = qkv_ref[...].astype(F32)
    off = HALO - (GDN_CONV - 1)
    y = jnp.zeros((c, GDN_CONV_CH), F32)
    for j in range(GDN_CONV):
        y = y + win_ref[off + j:off + j + c, :] * cw_ref[j:j + 1, :]
    y = _silu(y)
    win_ref[0:HALO, :] = win_ref[c:c + HALO, :]

    ab = ab_ref[...]
    g_all = -jnp.exp(alog_ref[...]) * jnp.logaddexp(ab + dtb_ref[...], 0.0)
    ii = lax.broadcasted_iota(jnp.int32, (c, c), 0)
    jj = lax.broadcasted_iota(jnp.int32, (c, c), 1)
    lower = ii >= jj
    strict = ii > jj
    tri = lower.astype(F32)
    eye = (ii == jj).astype(F32)
    gc_all = jnp.dot(tri, g_all, precision=lax.Precision.HIGHEST, preferred_element_type=F32)
    gc_t = gc_all.T
    beta_all = _sigmoid(ab)

    def l2n(t):
        return t * lax.rsqrt(jnp.sum(t * t, axis=-1, keepdims=True) + 1e-6)

    q_heads, k_heads, kt_heads = [], [], []
    for kh in range(GDN_K_HEADS):
        qh = l2n(y[:, kh * dh:(kh + 1) * dh]) * (dh ** -0.5)
        kh_ = l2n(y[:, GDN_KEY_WIDTH + kh * dh:GDN_KEY_WIDTH + (kh + 1) * dh])
        q_heads.append(qh)
        k_heads.append(kh_)
        kt_heads.append(kh_.T.astype(BF16))

    heads = range(GDN_V_HEADS)
    rep = GDN_V_HEADS // GDN_K_HEADS
    gcol = [gc_all[:, h:h + 1] for h in heads]
    beta = [beta_all[:, 8 + h:9 + h] for h in heads]
    decay = [jnp.where(lower, jnp.exp(jnp.where(lower, gcol[h] - gc_t[h:h + 1, :], 0.0)), 0.0) for h in heads]
    kb = [k_heads[h // rep] * beta[h] for h in heads]
    vb = [y[:, 2 * GDN_KEY_WIDTH + h * dh:2 * GDN_KEY_WIDTH + (h + 1) * dh] * beta[h] for h in heads]
    kq = [_dot(jnp.concatenate([kb[kh * rep + r].astype(BF16) for r in range(rep)]
                               + [q_heads[kh].astype(BF16)], axis=0), kt_heads[kh])
          for kh in range(GDN_K_HEADS)]
    lmat = [jnp.where(strict, kq[h // rep][(h % rep) * c:(h % rep + 1) * c] * decay[h], 0.0) for h in heads]
    qk = [kq[h // rep][rep * c:(rep + 1) * c] * decay[h] for h in heads]
    tinv = [eye - lm for lm in lmat]
    pw = [_split_dots([lm], lm)[0] for lm in lmat]
    for _ in range(4):
        both = [_split_dots([p, t], p) for p, t in zip(pw, tinv)]
        pw = [b[0] for b in both]
        tinv = [t + b[1] for t, b in zip(tinv, both)]
    tinv = [t + _split_dots([t], p)[0] for t, p in zip(tinv, pw)]
    eg = [jnp.exp(g) for g in gcol]
    uw = [_split_dots([tinv[h]], jnp.concatenate([vb[h], kb[h] * eg[h]], axis=1))[0] for h in heads]
    s_old = [state_ref[h] for h in heads]
    ws = [_dot(jnp.concatenate([uw[h][:, dh:].astype(BF16), (q_heads[h // rep] * eg[h]).astype(BF16)], axis=0),
               s_old[h].astype(BF16)) for h in heads]
    v_new = [uw[h][:, :dh] - ws[h][:c] for h in heads]
    o = [ws[h][c:] + _dot(qk[h].astype(BF16), v_new[h].astype(BF16)) for h in heads]
    for h in heads:
        glast = gcol[h][c - 1:c, :]
        vd = (v_new[h] * jnp.exp(glast - gcol[h])).astype(BF16)
        state_ref[h] = s_old[h] * jnp.exp(glast) + _dot(kt_heads[h // rep], vd)
    for h in heads:
        zh = z_ref[:, h * dh:(h + 1) * dh].astype(F32)
        on = o[h] * lax.rsqrt(jnp.mean(o[h] * o[h], axis=-1, keepdims=True) + RMS_EPS) * ng_ref[...]
        o_ref[:, h * dh:(h + 1) * dh] = (on * _silu(zh)).astype(o_ref.dtype)


def gated_deltanet(proj_a, proj_c, conv_w, a_log, dt_bias, norm_g, *, batch, seq):
    c = GDN_CHUNK
    nc = seq // c
    cw = jnp.zeros((HALO, GDN_CONV_CH), F32).at[:GDN_CONV].set(conv_w.astype(F32))
    pad = lambda v: jnp.zeros((1, LANES), F32).at[0, :GDN_V_HEADS].set(v.astype(F32))
    const = lambda b, s: (0, 0)
    return pl.pallas_call(
        _gdn_kernel,
        grid=(batch, nc),
        in_specs=[pl.BlockSpec((c, GDN_CONV_CH), lambda b, s: (b * nc + s, PA_QKV // GDN_CONV_CH)),
                  pl.BlockSpec((c, GDN_VAL_WIDTH), lambda b, s: (b * nc + s, PA_Z // GDN_VAL_WIDTH)),
                  pl.BlockSpec((c, LANES), lambda b, s: (b * nc + s, 0)),
                  pl.BlockSpec((HALO, GDN_CONV_CH), const),
                  pl.BlockSpec((1, LANES), const),
                  pl.BlockSpec((1, LANES), const),
                  pl.BlockSpec((1, GDN_HEAD_DIM), const)],
        out_specs=pl.BlockSpec((c, GDN_VAL_WIDTH), lambda b, s: (b * nc + s, 0)),
        out_shape=jax.ShapeDtypeStruct((batch * seq, GDN_VAL_WIDTH), BF16),
        scratch_shapes=[pltpu.VMEM((c + HALO, GDN_CONV_CH), F32),
                        pltpu.VMEM((GDN_V_HEADS, GDN_HEAD_DIM, GDN_HEAD_DIM), F32)],
        compiler_params=_cparams(("arbitrary", "arbitrary")),
        name="gated_deltanet",
    )(proj_a, proj_a, proj_c, cw, pad(a_log), pad(dt_bias), norm_g.reshape(1, GDN_HEAD_DIM).astype(F32))


def _rope_kernel(pos_ref, freq_ref, sign_ref, c_ref, s_ref):
    ang = pos_ref[...].astype(F32) * freq_ref[...]
    c_ref[...] = jnp.cos(ang)
    s_ref[...] = jnp.sin(ang) * sign_ref[...]


def rope_tables(positions, tm=512):
    t = positions.size
    inv_freq = ROPE_THETA ** (-jnp.arange(0, MLA_ROPE, 2, dtype=F32) / MLA_ROPE)
    pad = jnp.zeros((MLA_QKP - MLA_QK,), F32)
    freq = jnp.concatenate([jnp.zeros((MLA_NOPE,), F32), inv_freq, inv_freq, pad]).reshape(1, MLA_QKP)
    half = MLA_ROPE // 2
    sign = jnp.concatenate([jnp.zeros((MLA_NOPE,), F32), -jnp.ones((half,), F32),
                            jnp.ones((half,), F32), pad]).reshape(1, MLA_QKP)
    return pl.pallas_call(
        _rope_kernel,
        grid=(t // tm,),
        in_specs=[pl.BlockSpec((tm, 1), lambda i: (i, 0)),
                  pl.BlockSpec((1, MLA_QKP), lambda i: (0, 0)),
                  pl.BlockSpec((1, MLA_QKP), lambda i: (0, 0))],
        out_specs=[pl.BlockSpec((tm, MLA_QKP), lambda i: (i, 0))] * 2,
        out_shape=[jax.ShapeDtypeStruct((t, MLA_QKP), F32)] * 2,
        compiler_params=_cparams(("arbitrary",)),
        name="rope_tables",
    )(positions.reshape(t, 1).astype(jnp.int32), freq, sign)


def _mla_prep_kernel(cq_ref, ckv_ref, kr_ref, krs_ref, cos_ref, sin_ref, qn_ref, kvn_ref,
                     wq_ref, wqs_ref, wkv_ref, q_out, k_out, v_out):
    def rms(x, g):
        return x * lax.rsqrt(jnp.mean(x * x, axis=-1, keepdims=True) + RMS_EPS) * g

    cqn = rms(cq_ref[...].astype(F32), qn_ref[...]).astype(BF16)
    ckvn = rms(ckv_ref[...].astype(F32), kvn_ref[...]).astype(BF16)
    cos = cos_ref[...]
    sin = sin_ref[...]
    kpe = (kr_ref[...] * cos[:, MLA_NOPE:] + krs_ref[...] * sin[:, MLA_NOPE:]).astype(k_out.dtype)
    q_all = _dot(cqn, wq_ref[...])
    qs_all = _dot(cqn, wqs_ref[...])
    kv_all = _dot(ckvn, wkv_ref[...])
    scale = (MLA_QK ** -0.5) * math.log2(math.e)
    for h in range(MLA_HEADS):
        cols = slice(h * MLA_QKP, (h + 1) * MLA_QKP)
        q_out[h] = ((q_all[:, cols] * cos + qs_all[:, cols] * sin) * scale).astype(q_out.dtype)
        base = h * (MLA_NOPE + MLA_V)
        k_out[h, :, :MLA_NOPE] = kv_all[:, base:base + MLA_NOPE].astype(k_out.dtype)
        k_out[h, :, MLA_NOPE:] = kpe
        v_out[h] = kv_all[:, base + MLA_NOPE:base + MLA_NOPE + MLA_V].astype(v_out.dtype)


def mla_prep(proj_a, proj_c, cos_t, sin_t, q_norm, kv_norm, w_uq, w_ukv, tm=512):
    t = proj_a.shape[0]
    hh = MLA_HEADS
    wq = w_uq.reshape(MLA_Q_RANK, hh, MLA_QK)
    half = MLA_ROPE // 2
    zeros = lambda n: jnp.zeros((MLA_Q_RANK, hh, n), w_uq.dtype)
    padw = lambda w: w.reshape(MLA_Q_RANK, hh * MLA_QKP).astype(BF16)
    wq_pad = padw(jnp.concatenate([wq, zeros(MLA_QKP - MLA_QK)], axis=-1))
    wq_swap = padw(jnp.concatenate([zeros(MLA_NOPE), wq[:, :, MLA_NOPE + half:],
                                    wq[:, :, MLA_NOPE:MLA_NOPE + half], zeros(MLA_QKP - MLA_QK)], axis=-1))
    row = lambda v: v.reshape(1, -1).astype(F32)
    const = lambda i: (0, 0)
    heads_tile = lambda i: (0, i, 0)
    return pl.pallas_call(
        _mla_prep_kernel,
        grid=(t // tm,),
        in_specs=[pl.BlockSpec((tm, MLA_Q_RANK), lambda i: (i, PA_CQ // MLA_Q_RANK)),
                  pl.BlockSpec((tm, MLA_KV_RANK), lambda i: (i, PA_CKV // MLA_KV_RANK)),
                  pl.BlockSpec((tm, LANES), lambda i: (i, 1)),
                  pl.BlockSpec((tm, LANES), lambda i: (i, 2)),
                  pl.BlockSpec((tm, MLA_QKP), lambda i: (i, 0)),
                  pl.BlockSpec((tm, MLA_QKP), lambda i: (i, 0)),
                  pl.BlockSpec((1, MLA_Q_RANK), const),
                  pl.BlockSpec((1, MLA_KV_RANK), const),
                  pl.BlockSpec((MLA_Q_RANK, hh * MLA_QKP), const),
                  pl.BlockSpec((MLA_Q_RANK, hh * MLA_QKP), const),
                  pl.BlockSpec((MLA_KV_RANK, hh * (MLA_NOPE + MLA_V)), const)],
        out_specs=[pl.BlockSpec((hh, tm, MLA_QKP), heads_tile),
                   pl.BlockSpec((hh, tm, MLA_QKP), heads_tile),
                   pl.BlockSpec((hh, tm, MLA_V), heads_tile)],
        out_shape=[jax.ShapeDtypeStruct((hh, t, MLA_QKP), BF16),
                   jax.ShapeDtypeStruct((hh, t, MLA_QKP), BF16),
                   jax.ShapeDtypeStruct((hh, t, MLA_V), BF16)],
        compiler_params=_cparams(("arbitrary",)),
        name="mla_prep",
    )(proj_a, proj_a, proj_c, proj_c, cos_t, sin_t, row(q_norm), row(kv_norm),
      wq_pad, wq_swap, w_ukv.astype(BF16))


def _flash_kernel(qi_tab, ki_tab, q_ref, k_ref, v_ref, o_ref, m_ref, acc_ref, *, tq, tk):
    p = pl.program_id(2)
    qi = qi_tab[p]
    ki = ki_tab[p]
    last_k = (qi + 1) * (tq // tk) - 1

    @pl.when(ki == 0)
    def _():
        m_ref[...] = jnp.full_like(m_ref, -jnp.inf)
        acc_ref[...] = jnp.zeros_like(acc_ref)

    s = _dot_nt(q_ref[0], k_ref[0])
    v_ext = jnp.concatenate([v_ref[0], jnp.ones((tk, LANES), BF16)], axis=1)

    def update(sc):
        m_old = m_ref[...]
        m_new = jnp.maximum(m_old, jnp.max(sc, axis=-1, keepdims=True))
        alpha = jnp.exp2(m_old - m_new)
        pr = jnp.exp2(sc - m_new).astype(BF16)
        acc_ref[...] = alpha * acc_ref[...] + _dot(pr, v_ext)
        m_ref[...] = m_new

    needs_mask = (ki + 1) * tk > qi * tq + 1

    @pl.when(jnp.logical_not(needs_mask))
    def _():
        update(s)

    @pl.when(needs_mask)
    def _():
        row = qi * tq + lax.broadcasted_iota(jnp.int32, (tq, tk), 0)
        col = ki * tk + lax.broadcasted_iota(jnp.int32, (tq, tk), 1)
        update(jnp.where(col <= row, s, -jnp.inf))

    @pl.when(ki == last_k)
    def _():
        acc = acc_ref[...]
        o_ref[...] = (acc[:, :MLA_V] / acc[:, MLA_V:MLA_V + 1]).astype(o_ref.dtype)


def flash_attention(q, k, v, *, batch, seq, tq=1024, tk=1024):
    nq = seq // tq
    nk = seq // tk
    pairs = [(a, b) for a in range(nq) for b in range((a + 1) * (tq // tk))]
    qi_tab = jnp.asarray([p[0] for p in pairs], jnp.int32)
    ki_tab = jnp.asarray([p[1] for p in pairs], jnp.int32)
    grid_spec = pltpu.PrefetchScalarGridSpec(
        num_scalar_prefetch=2,
        grid=(batch, MLA_HEADS, len(pairs)),
        in_specs=[pl.BlockSpec((1, tq, MLA_QKP), lambda b, h, p, qt, kt: (h, b * nq + qt[p], 0)),
                  pl.BlockSpec((1, tk, MLA_QKP), lambda b, h, p, qt, kt: (h, b * nk + kt[p], 0)),
                  pl.BlockSpec((1, tk, MLA_V), lambda b, h, p, qt, kt: (h, b * nk + kt[p], 0))],
        out_specs=pl.BlockSpec((tq, MLA_V), lambda b, h, p, qt, kt: (b * nq + qt[p], h)),
        scratch_shapes=[pltpu.VMEM((tq, 1), F32), pltpu.VMEM((tq, MLA_V + LANES), F32)],
    )
    return pl.pallas_call(
        functools.partial(_flash_kernel, tq=tq, tk=tk),
        grid_spec=grid_spec,
        out_shape=jax.ShapeDtypeStruct((batch * seq, MLA_HEADS * MLA_V), BF16),
        compiler_params=_cparams(("arbitrary", "arbitrary", "arbitrary")),
        name="flash_attention",
    )(qi_tab, ki_tab, q, k, v)


def _merge_kernel(ha_ref, hb_ref, hc_ref, hd_ref, ga_ref, gb_ref, gc_ref, gd_ref, w_ref, o_ref):
    acc = None
    for b, (h_ref, g_ref) in enumerate(((ha_ref, ga_ref), (hb_ref, gb_ref), (hc_ref, gc_ref), (hd_ref, gd_ref))):
        term = g_ref[...].astype(F32) * _dot(h_ref[...], w_ref[b])
        acc = term if acc is None else acc + term
    o_ref[...] = acc.astype(o_ref.dtype)


def merge_branches(h_a, h_b, h_c, h_d, proj_a, w_branch, tm=512, tn=512):
    t = h_a.shape[0]
    width = h_a.shape[1]
    g0 = PA_GATE // tn
    per = D_MODEL // tn
    hspec = pl.BlockSpec((tm, width), lambda i, j: (i, 0))
    gspecs = [pl.BlockSpec((tm, tn), functools.partial(lambda i, j, b: (i, g0 + b * per + j), b=b))
              for b in range(N_BRANCH)]
    return pl.pallas_call(
        _merge_kernel,
        grid=(t // tm, D_MODEL // tn),
        in_specs=[hspec] * 4 + gspecs + [pl.BlockSpec((N_BRANCH, width, tn), lambda i, j: (0, 0, j))],
        out_specs=pl.BlockSpec((tm, tn), lambda i, j: (i, j)),
        out_shape=jax.ShapeDtypeStruct((t, D_MODEL), BF16),
        compiler_params=_cparams(("arbitrary", "arbitrary")),
        name="merge_branches",
    )(h_a, h_b, h_c, h_d, proj_a, proj_a, proj_a, proj_a, w_branch)


def _ffn_up_kernel(x_ref, wg_ref, wu_ref, cwg_ref, cwu_ref, bg_ref, bu_ref, o_ref, gwin_ref, uwin_ref, *, tm):
    s = pl.program_id(2)

    @pl.when(s == 0)
    def _():
        gwin_ref[0:HALO, :] = jnp.zeros((HALO, gwin_ref.shape[1]), F32)
        uwin_ref[0:HALO, :] = jnp.zeros((HALO, uwin_ref.shape[1]), F32)

    x = x_ref[...]
    gwin_ref[HALO:HALO + tm, :] = _dot(x, wg_ref[...])
    uwin_ref[HALO:HALO + tm, :] = _dot(x, wu_ref[...])
    off = HALO - (FFN_CONV - 1)

    def conv(win_ref, cw_ref, b_ref):
        acc = b_ref[...] + win_ref[off:off + tm, :] * cw_ref[0:1, :]
        for j in range(1, FFN_CONV):
            acc = acc + win_ref[off + j:off + j + tm, :] * cw_ref[j:j + 1, :]
        return acc

    gate = conv(gwin_ref, cwg_ref, bg_ref)
    up = conv(uwin_ref, cwu_ref, bu_ref)
    o_ref[...] = (_silu(gate) * up).astype(o_ref.dtype)
    gwin_ref[0:HALO, :] = gwin_ref[tm:tm + HALO, :]
    uwin_ref[0:HALO, :] = uwin_ref[tm:tm + HALO, :]


def ffn_up(hn, w_up, conv_w, conv_b, *, batch, seq, tm=1024, tn=512):
    ns = seq // tm
    wg = w_up[:, :FFN_DIM].astype(BF16)
    wu = w_up[:, FFN_DIM:].astype(BF16)
    padw = lambda w: jnp.zeros((HALO, FFN_DIM), F32).at[:FFN_CONV].set(w.astype(F32))
    row = lambda v: v.reshape(1, FFN_DIM).astype(F32)
    wspec = pl.BlockSpec((D_MODEL, tn), lambda j, b, s: (0, j))
    cspec = pl.BlockSpec((HALO, tn), lambda j, b, s: (0, j))
    bspec = pl.BlockSpec((1, tn), lambda j, b, s: (0, j))
    return pl.pallas_call(
        functools.partial(_ffn_up_kernel, tm=tm),
        grid=(FFN_DIM // tn, batch, ns),
        in_specs=[pl.BlockSpec((tm, D_MODEL), lambda j, b, s: (b * ns + s, 0)),
                  wspec, wspec, cspec, cspec, bspec, bspec],
        out_specs=pl.BlockSpec((tm, tn), lambda j, b, s: (b * ns + s, j)),
        out_shape=jax.ShapeDtypeStruct((batch * seq, FFN_DIM), BF16),
        scratch_shapes=[pltpu.VMEM((tm + HALO, tn), F32), pltpu.VMEM((tm + HALO, tn), F32)],
        compiler_params=_cparams(("arbitrary", "arbitrary", "arbitrary")),
        name="ffn_up",
    )(hn, wg, wu, padw(conv_w[:, :FFN_DIM]), padw(conv_w[:, FFN_DIM:]),
      row(conv_b[:FFN_DIM]), row(conv_b[FFN_DIM:]))


def _in_proj_weights(w_in):
    o = np.cumsum([0, 1024, 512, 512, 1024, 1024, 8, 8, 2048, 512, 512, 64, 8192])
    col = lambda i: w_in[:, int(o[i]):int(o[i + 1])]
    pool, q, k, v, z, a, b, conf, cq, ckv, krope, gates = (col(i) for i in range(12))
    w_a = jnp.concatenate([q, k, v, conf, pool, z, cq, ckv, gates], axis=1).astype(BF16)
    half = MLA_ROPE // 2
    zeros = lambda n: jnp.zeros((D_MODEL, n), w_in.dtype)
    w_c = jnp.concatenate([a, b, zeros(112), krope, zeros(64),
                           krope[:, half:], krope[:, :half], zeros(64)], axis=1).astype(BF16)
    return w_a, w_c


def kernel(x, positions, mix_norm, w_in, pool_w, pool_scale, gdn_conv_w, gdn_a_log, gdn_dt_bias, gdn_norm, conf_conv_w, conf_conv_b, conf_ln_g, conf_ln_b, mla_q_norm, mla_w_uq, mla_kv_norm, mla_w_ukv, w_pool_out, w_gdn_out, w_conf_out, w_mla_out, w_out, ffn_norm, ffn_w_up, ffn_conv_w, ffn_conv_b, ffn_w_down, final_norm):
    batch, seq, d = x.shape
    t = batch * seq
    xf = x.reshape(t, d)
    cos_t, sin_t = rope_tables(positions)
    xn = rmsnorm(xf, mix_norm[0], BF16)
    for l in range(DEPTH):
        w_a, w_c = _in_proj_weights(w_in[l])
        proj_a = matmul(xn, w_a, tm=1024, tn=1024, tk=D_MODEL, out_dtype=BF16,
                        sigmoid_from=PA_GATE // 1024, name="in_proj")
        proj_c = matmul(xn, w_c, tm=1024, tn=PC_WIDTH, tk=D_MODEL, out_dtype=F32, name="in_proj_small")
        h_a = pool_mixer(proj_a, pool_w[l], pool_scale[l], batch=batch, seq=seq)
        h_b = gated_deltanet(proj_a, proj_c, gdn_conv_w[l], gdn_a_log[l], gdn_dt_bias[l], gdn_norm[l],
                             batch=batch, seq=seq)
        h_c = conformer_conv(proj_a, conf_conv_w[l], conf_conv_b[l], conf_ln_g[l], conf_ln_b[l],
                             batch=batch, seq=seq)
        q, k, v = mla_prep(proj_a, proj_c, cos_t, sin_t, mla_q_norm[l], mla_kv_norm[l],
                           mla_w_uq[l], mla_w_ukv[l])
        h_d = flash_attention(q, k, v, batch=batch, seq=seq)
        w_branch = jnp.stack([w_pool_out[l], w_gdn_out[l], w_conf_out[l], w_mla_out[l]]).astype(BF16)
        merged = merge_branches(h_a, h_b, h_c, h_d, proj_a, w_branch)
        xf, hn = matmul(merged, w_out[l].astype(BF16), tm=512, tn=D_MODEL, tk=D_MODEL, out_dtype=F32,
                        residual=xf, norm="also", norm_gain=ffn_norm[l], norm_dtype=BF16, name="out_proj")
        mid = ffn_up(hn, ffn_w_up[l], ffn_conv_w[l], ffn_conv_b[l], batch=batch, seq=seq)
        w_down = ffn_w_down[l].astype(BF16)
        if l + 1 < DEPTH:
            xf, xn = matmul(mid, w_down, tm=512, tn=D_MODEL, tk=512, out_dtype=F32, residual=xf,
                            norm="also", norm_gain=mix_norm[l + 1], norm_dtype=BF16, name="ffn_down")
        else:
            out = matmul(mid, w_down, tm=512, tn=D_MODEL, tk=512, out_dtype=F32, residual=xf,
                         norm="only", norm_gain=final_norm, name="ffn_down_final")
    return out.reshape(batch, seq, d)
```

```python
import functools
import math

import numpy as np
import jax
import jax.numpy as jnp
from jax import lax
from jax.experimental import pallas as pl
from jax.experimental.pallas import tpu as pltpu

F32 = jnp.float32
BF16 = jnp.bfloat16

D_MODEL = 2048
DEPTH = 2
POOL_WINDOWS = (2, 4, 8, 16)
POOL_GROUPS = 4
POOL_GROUP_DIM = 256
POOL_WIDTH = 1024
GDN_K_HEADS = 4
GDN_V_HEADS = 8
GDN_HEAD_DIM = 128
GDN_KEY_WIDTH = 512
GDN_VAL_WIDTH = 1024
GDN_CONV_CH = 2048
GDN_CONV = 4
GDN_CHUNK = 64
CONF_WIDTH = 1024
CONF_CONV = 31
MLA_HEADS = 8
MLA_NOPE = 128
MLA_ROPE = 64
MLA_V = 128
MLA_QK = MLA_NOPE + MLA_ROPE
MLA_QKP = 256
MLA_Q_RANK = 512
MLA_KV_RANK = 512
ROPE_THETA = 10000.0
N_BRANCH = 4
FFN_DIM = 5632
FFN_CONV = 3
RMS_EPS = 1e-6
LN_EPS = 1e-5

VMEM_LIMIT_BYTES = 56 * 1024 * 1024
LANES = 128
HALO = 8
POOL_HALO = 16
CONF_HALO = 32

PA_QKV, PA_CONF, PA_POOL, PA_Z, PA_CQ, PA_CKV, PA_GATE = 0, 2048, 4096, 5120, 6144, 6656, 7168
PA_WIDTH = PA_GATE + N_BRANCH * D_MODEL
PC_WIDTH = 384


def _cparams(semantics):
    return pltpu.CompilerParams(dimension_semantics=semantics, vmem_limit_bytes=VMEM_LIMIT_BYTES)


def _sigmoid(x):
    return jax.nn.sigmoid(x)


def _silu(x):
    return x * jax.nn.sigmoid(x)


def _dot(a, b):
    return jnp.dot(a, b, preferred_element_type=F32)


def _dot_nt(a, b):
    return lax.dot_general(a, b, (((1,), (1,)), ((), ())), preferred_element_type=F32)


def _rmsnorm_kernel(x_ref, g_ref, o_ref):
    x = x_ref[...]
    ms = jnp.mean(x * x, axis=-1, keepdims=True)
    o_ref[...] = (x * lax.rsqrt(ms + RMS_EPS) * g_ref[...]).astype(o_ref.dtype)


def rmsnorm(x, g, out_dtype, tm=512):
    m, d = x.shape
    return pl.pallas_call(
        _rmsnorm_kernel,
        grid=(m // tm,),
        in_specs=[pl.BlockSpec((tm, d), lambda i: (i, 0)), pl.BlockSpec((1, d), lambda i: (0, 0))],
        out_specs=pl.BlockSpec((tm, d), lambda i: (i, 0)),
        out_shape=jax.ShapeDtypeStruct((m, d), out_dtype),
        compiler_params=_cparams(("arbitrary",)),
        name="rmsnorm",
    )(x, g.reshape(1, d).astype(F32))


def _mm_kernel(*refs, nk, has_res, has_norm):
    refs = list(refs)
    a_ref, w_ref = refs[:2]
    pos = 2
    r_ref = g_ref = None
    if has_res:
        r_ref = refs[pos]
        pos += 1
    if has_norm:
        g_ref = refs[pos]
        pos += 1
    outs = refs[pos:-1]
    acc_ref = refs[-1]
    k = pl.program_id(2)
    part = _dot(a_ref[...], w_ref[...])

    def finish(acc):
        if r_ref is not None:
            acc = acc + r_ref[...]
        outs[0][...] = acc.astype(outs[0].dtype)
        if has_norm:
            y = acc * lax.rsqrt(jnp.mean(acc * acc, axis=-1, keepdims=True) + RMS_EPS) * g_ref[...]
            outs[1][...] = y.astype(outs[1].dtype)

    if nk == 1:
        finish(part)
    else:
        @pl.when(k == 0)
        def _():
            acc_ref[...] = part

        @pl.when(jnp.logical_and(k > 0, k < nk - 1))
        def _():
            acc_ref[...] += part

        @pl.when(k == nk - 1)
        def _():
            finish(acc_ref[...] + part)


def matmul(a, w, *, tm, tn, tk, out_dtype, residual=None, norm_gain=None, norm_dtype=None, name="matmul"):
    m, kdim = a.shape
    n = w.shape[1]
    nk = kdim // tk
    in_specs = [pl.BlockSpec((tm, tk), lambda i, j, k: (i, k)),
                pl.BlockSpec((tk, tn), lambda i, j, k: (k, j))]
    args = [a, w]
    if residual is not None:
        in_specs.append(pl.BlockSpec((tm, tn), lambda i, j, k: (i, j)))
        args.append(residual)
    out_spec = pl.BlockSpec((tm, tn), lambda i, j, k: (i, j))
    out_specs, out_shape = out_spec, jax.ShapeDtypeStruct((m, n), out_dtype)
    if norm_gain is not None:
        assert tn == n
        in_specs.append(pl.BlockSpec((1, n), lambda i, j, k: (0, 0)))
        args.append(norm_gain.reshape(1, n).astype(F32))
        out_specs = [out_spec, out_spec]
        out_shape = [out_shape, jax.ShapeDtypeStruct((m, n), norm_dtype)]
    return pl.pallas_call(
        functools.partial(_mm_kernel, nk=nk, has_res=residual is not None, has_norm=norm_gain is not None),
        grid=(m // tm, n // tn, nk),
        in_specs=in_specs,
        out_specs=out_specs,
        out_shape=out_shape,
        scratch_shapes=[pltpu.VMEM((tm, tn) if nk > 1 else (HALO, LANES), F32)],
        compiler_params=_cparams(("arbitrary", "arbitrary", "arbitrary")),
        name=name,
    )(*args)


def _in_proj_kernel(x_ref, g_ref, w_ref, wc_ref, o_ref, oc_ref, xn_ref, *, sigmoid_from):
    j = pl.program_id(1)

    @pl.when(j == 0)
    def _():
        x = x_ref[...]
        xn = (x * lax.rsqrt(jnp.mean(x * x, axis=-1, keepdims=True) + RMS_EPS) * g_ref[...]).astype(BF16)
        xn_ref[...] = xn
        oc_ref[...] = _dot(xn, wc_ref[...])

    acc = _dot(xn_ref[...], w_ref[...])

    @pl.when(j >= sigmoid_from)
    def _():
        o_ref[...] = _sigmoid(acc).astype(o_ref.dtype)

    @pl.when(j < sigmoid_from)
    def _():
        o_ref[...] = acc.astype(o_ref.dtype)


def in_proj(x, gain, w, w_small, *, tm, tn, sigmoid_from):
    m, d = x.shape
    n = w.shape[1]
    ns = w_small.shape[1]
    return pl.pallas_call(
        functools.partial(_in_proj_kernel, sigmoid_from=sigmoid_from),
        grid=(m // tm, n // tn),
        in_specs=[pl.BlockSpec((tm, d), lambda i, j: (i, 0)),
                  pl.BlockSpec((1, d), lambda i, j: (0, 0)),
                  pl.BlockSpec((d, tn), lambda i, j: (0, j)),
                  pl.BlockSpec((d, ns), lambda i, j: (0, 0))],
        out_specs=[pl.BlockSpec((tm, tn), lambda i, j: (i, j)),
                   pl.BlockSpec((tm, ns), lambda i, j: (i, 0))],
        out_shape=[jax.ShapeDtypeStruct((m, n), BF16), jax.ShapeDtypeStruct((m, ns), F32)],
        scratch_shapes=[pltpu.VMEM((tm, d), BF16)],
        compiler_params=_cparams(("arbitrary", "arbitrary")),
        name="in_proj",
    )(x, gain.reshape(1, d).astype(F32), w, w_small)


def _pool_bands(ts):
    i = np.arange(ts)[:, None]
    m = np.arange(ts + POOL_HALO)[None, :]
    lag = (i + POOL_HALO) - m
    return np.stack([((lag >= 0) & (lag < w)) for w in POOL_WINDOWS]).astype(np.float32)


def _pool_kernel(u_ref, band_ref, pw_ref, scale_ref, o_ref, tail_ref, *, ts):
    s = pl.program_id(1)

    @pl.when(s == 0)
    def _():
        tail_ref[...] = jnp.zeros_like(tail_ref)

    t_abs = s * ts + lax.broadcasted_iota(jnp.int32, (ts, 1), 0)
    for g, win_len in enumerate(POOL_WINDOWS):
        cols = slice(g * POOL_GROUP_DIM, (g + 1) * POOL_GROUP_DIM)
        ug = u_ref[:, cols]
        win = jnp.concatenate([tail_ref[:, cols], ug], axis=0)
        sums = _dot(band_ref[g], win)
        cnt = jnp.minimum(t_abs + 1, win_len).astype(F32)
        diff = sums / cnt - ug.astype(F32)
        y = _dot(diff.astype(BF16), pw_ref[g]) * scale_ref[:, cols]
        o_ref[:, cols] = y.astype(o_ref.dtype)
    tail_ref[...] = u_ref[ts - POOL_HALO:, :]


def pool_mixer(proj_a, pool_w, pool_scale, *, batch, seq, ts=512):
    ns = seq // ts
    bands = jnp.asarray(_pool_bands(ts), BF16)
    cb = PA_POOL // POOL_WIDTH
    return pl.pallas_call(
        functools.partial(_pool_kernel, ts=ts),
        grid=(batch, ns),
        in_specs=[pl.BlockSpec((ts, POOL_WIDTH), lambda b, s: (b * ns + s, cb)),
                  pl.BlockSpec((POOL_GROUPS, ts, ts + POOL_HALO), lambda b, s: (0, 0, 0)),
                  pl.BlockSpec((POOL_GROUPS, POOL_GROUP_DIM, POOL_GROUP_DIM), lambda b, s: (0, 0, 0)),
                  pl.BlockSpec((1, POOL_WIDTH), lambda b, s: (0, 0))],
        out_specs=pl.BlockSpec((ts, POOL_WIDTH), lambda b, s: (b * ns + s, 0)),
        out_shape=jax.ShapeDtypeStruct((batch * seq, POOL_WIDTH), BF16),
        scratch_shapes=[pltpu.VMEM((POOL_HALO, POOL_WIDTH), BF16)],
        compiler_params=_cparams(("arbitrary", "arbitrary")),
        name="pool_mixer",
    )(proj_a, bands, pool_w.astype(BF16), pool_scale.reshape(1, POOL_WIDTH).astype(F32))


def _conf_kernel(u_ref, cw_ref, cb_ref, lg_ref, lb_ref, o_ref, win_ref, sh_ref, y_ref, *, ts, rc, rn):
    s = pl.program_id(1)
    span = ts + CONF_HALO - HALO

    @pl.when(s == 0)
    def _():
        win_ref[0:CONF_HALO, :] = jnp.zeros((CONF_HALO, CONF_WIDTH), F32)

    a = u_ref[:, :CONF_WIDTH].astype(F32)
    gate = u_ref[:, CONF_WIDTH:].astype(F32)
    win_ref[CONF_HALO:CONF_HALO + ts, :] = a * _sigmoid(gate)
    for r in range(1, HALO):
        sh_ref[r - 1, 0:span, :] = win_ref[r:r + span, :]
    off = CONF_HALO - (CONF_CONV - 1)

    for base in range(0, ts, rc):
        for c in range(CONF_WIDTH // LANES):
            lanes = slice(c * LANES, (c + 1) * LANES)
            acc = None
            for j in range(CONF_CONV):
                start = base + ((off + j) // HALO) * HALO
                r = (off + j) % HALO
                x = win_ref[start:start + rc, lanes] if r == 0 else sh_ref[r - 1, start:start + rc, lanes]
                term = x * cw_ref[j:j + 1, lanes]
                acc = term if acc is None else acc + term
            y_ref[base:base + rc, lanes] = acc + cb_ref[:, lanes]

    def norm_rows(i, carry):
        rows = pl.ds(pl.multiple_of(i * rn, rn), rn)
        h = y_ref[rows, :]
        mu = jnp.mean(h, axis=-1, keepdims=True)
        hc = h - mu
        var = jnp.mean(hc * hc, axis=-1, keepdims=True)
        y = hc * lax.rsqrt(var + LN_EPS) * lg_ref[...] + lb_ref[...]
        o_ref[rows, :] = _silu(y).astype(o_ref.dtype)
        return carry

    lax.fori_loop(0, ts // rn, norm_rows, 0, unroll=4)
    win_ref[0:CONF_HALO, :] = win_ref[ts:ts + CONF_HALO, :]


def conformer_conv(proj_a, conv_w, conv_b, ln_g, ln_b, *, batch, seq, ts=256, rc=128, rn=32):
    ns = seq // ts
    cw = jnp.zeros((CONF_HALO, CONF_WIDTH), F32).at[:CONF_CONV].set(conv_w.astype(F32))
    row = lambda v: v.reshape(1, CONF_WIDTH).astype(F32)
    cb = PA_CONF // (2 * CONF_WIDTH)
    const = lambda b, s: (0, 0)
    return pl.pallas_call(
        functools.partial(_conf_kernel, ts=ts, rc=rc, rn=rn),
        grid=(batch, ns),
        in_specs=[pl.BlockSpec((ts, 2 * CONF_WIDTH), lambda b, s: (b * ns + s, cb)),
                  pl.BlockSpec((CONF_HALO, CONF_WIDTH), const),
                  pl.BlockSpec((1, CONF_WIDTH), const),
                  pl.BlockSpec((1, CONF_WIDTH), const),
                  pl.BlockSpec((1, CONF_WIDTH), const)],
        out_specs=pl.BlockSpec((ts, CONF_WIDTH), lambda b, s: (b * ns + s, 0)),
        out_shape=jax.ShapeDtypeStruct((batch * seq, CONF_WIDTH), BF16),
        scratch_shapes=[pltpu.VMEM((ts + CONF_HALO, CONF_WIDTH), F32),
                        pltpu.VMEM((HALO - 1, ts + CONF_HALO - HALO, CONF_WIDTH), F32),
                        pltpu.VMEM((ts, CONF_WIDTH), F32)],
        compiler_params=_cparams(("arbitrary", "arbitrary")),
        name="conformer_conv",
    )(proj_a, cw, row(conv_b), row(ln_g), row(ln_b))


def _split(a):
    hi = a.astype(BF16)
    return hi, (a - hi.astype(F32)).astype(BF16)


def _split_dots(lhs_list, rhs):
    r_hi, r_lo = _split(rhs)
    parts = [_split(l) for l in lhs_list]
    his = [p[0] for p in parts]
    los = [p[1] for p in parts]
    n = len(lhs_list)
    m = lhs_list[0].shape[0]
    top = _dot(jnp.concatenate(his + los, axis=0), r_hi)
    bot = _dot(jnp.concatenate(his, axis=0) if n > 1 else his[0], r_lo)
    return [top[i * m:(i + 1) * m] + (top[(n + i) * m:(n + i + 1) * m] + bot[i * m:(i + 1) * m])
            for i in range(n)]


def _gdn_kernel(qkv_ref, z_ref, ab_ref, cw_ref, alog_ref, dtb_ref, ng_ref, o_ref, win_ref, state_ref):
    c = GDN_CHUNK
    dh = GDN_HEAD_DIM
    step = pl.program_id(1)

    @pl.when(step == 0)
    def _():
        win_ref[0:HALO, :] = jnp.zeros((HALO, GDN_CONV_CH), F32)
        state_ref[...] = jnp.zeros_like(state_ref)

    win_ref[HALO:HALO + c, :] = qkv_ref[...].astype(F32)
    off = HALO - (GDN_CONV - 1)
    y = jnp.zeros((c, GDN_CONV_CH), F32)
    for j in range(GDN_CONV):
        y = y + win_ref[off + j:off + j + c, :] * cw_ref[j:j + 1, :]
    y = _silu(y)
    win_ref[0:HALO, :] = win_ref[c:c + HALO, :]

    ab = ab_ref[...]
    g_all = -jnp.exp(alog_ref[...]) * jnp.logaddexp(ab + dtb_ref[...], 0.0)
    ii = lax.broadcasted_iota(jnp.int32, (c, c), 0)
    jj = lax.broadcasted_iota(jnp.int32, (c, c), 1)
    lower = ii >= jj
    strict = ii > jj
    tri = lower.astype(F32)
    eye = (ii == jj).astype(F32)
    gc_all = jnp.dot(tri, g_all, precision=lax.Precision.HIGHEST, preferred_element_type=F32)
    gc_t = gc_all.T
    beta_all = _sigmoid(ab)

    def l2n(t):
        return t * lax.rsqrt(jnp.sum(t * t, axis=-1, keepdims=True) + 1e-6)

    q_heads, k_heads, kt_heads = [], [], []
    for kh in range(GDN_K_HEADS):
        qh = l2n(y[:, kh * dh:(kh + 1) * dh]) * (dh ** -0.5)
        kh_ = l2n(y[:, GDN_KEY_WIDTH + kh * dh:GDN_KEY_WIDTH + (kh + 1) * dh])
        q_heads.append(qh)
        k_heads.append(kh_)
        kt_heads.append(kh_.T.astype(BF16))

    heads = range(GDN_V_HEADS)
    rep = GDN_V_HEADS // GDN_K_HEADS
    gcol = [gc_all[:, h:h + 1] for h in heads]
    beta = [beta_all[:, 8 + h:9 + h] for h in heads]
    decay = [jnp.where(lower, jnp.exp(jnp.where(lower, gcol[h] - gc_t[h:h + 1, :], 0.0)), 0.0) for h in heads]
    kb = [k_heads[h // rep] * beta[h] for h in heads]
    vb = [y[:, 2 * GDN_KEY_WIDTH + h * dh:2 * GDN_KEY_WIDTH + (h + 1) * dh] * beta[h] for h in heads]
    kq = [_dot(jnp.concatenate([kb[kh * rep + r].astype(BF16) for r in range(rep)]
                               + [q_heads[kh].astype(BF16)], axis=0), kt_heads[kh])
          for kh in range(GDN_K_HEADS)]
    lmat = [jnp.where(strict, kq[h // rep][(h % rep) * c:(h % rep + 1) * c] * decay[h], 0.0) for h in heads]
    qk = [kq[h // rep][rep * c:(rep + 1) * c] * decay[h] for h in heads]
    tinv = [eye - lm for lm in lmat]
    pw = [_split_dots([lm], lm)[0] for lm in lmat]
    for _ in range(4):
        both = [_split_dots([p, t], p) for p, t in zip(pw, tinv)]
        pw = [b[0] for b in both]
        tinv = [t + b[1] for t, b in zip(tinv, both)]
    tinv = [t + _split_dots([t], p)[0] for t, p in zip(tinv, pw)]
    eg = [jnp.exp(g) for g in gcol]
    uw = [_split_dots([tinv[h]], jnp.concatenate([vb[h], kb[h] * eg[h]], axis=1))[0] for h in heads]
    s_old = [state_ref[h] for h in heads]
    ws = [_dot(jnp.concatenate([uw[h][:, dh:].astype(BF16), (q_heads[h // rep] * eg[h]).astype(BF16)], axis=0),
               s_old[h].astype(BF16)) for h in heads]
    v_new = [uw[h][:, :dh] - ws[h][:c] for h in heads]
    o = [ws[h][c:] + _dot(qk[h].astype(BF16), v_new[h].astype(BF16)) for h in heads]
    for h in heads:
        glast = gcol[h][c - 1:c, :]
        vd = (v_new[h] * jnp.exp(glast - gcol[h])).astype(BF16)
        state_ref[h] = s_old[h] * jnp.exp(glast) + _dot(kt_heads[h // rep], vd)
    for h in heads:
        zh = z_ref[:, h * dh:(h + 1) * dh].astype(F32)
        on = o[h] * lax.rsqrt(jnp.mean(o[h] * o[h], axis=-1, keepdims=True) + RMS_EPS) * ng_ref[...]
        o_ref[:, h * dh:(h + 1) * dh] = (on * _silu(zh)).astype(o_ref.dtype)


def gated_deltanet(proj_a, proj_c, conv_w, a_log, dt_bias, norm_g, *, batch, seq):
    c = GDN_CHUNK
    nc = seq // c
    cw = jnp.zeros((HALO, GDN_CONV_CH), F32).at[:GDN_CONV].set(conv_w.astype(F32))
    pad = lambda v: jnp.zeros((1, LANES), F32).at[0, :GDN_V_HEADS].set(v.astype(F32))
    const = lambda b, s: (0, 0)
    return pl.pallas_call(
        _gdn_kernel,
        grid=(batch, nc),
        in_specs=[pl.BlockSpec((c, GDN_CONV_CH), lambda b, s: (b * nc + s, PA_QKV // GDN_CONV_CH)),
                  pl.BlockSpec((c, GDN_VAL_WIDTH), lambda b, s: (b * nc + s, PA_Z // GDN_VAL_WIDTH)),
                  pl.BlockSpec((c, LANES), lambda b, s: (b * nc + s, 0)),
                  pl.BlockSpec((HALO, GDN_CONV_CH), const),
                  pl.BlockSpec((1, LANES), const),
                  pl.BlockSpec((1, LANES), const),
                  pl.BlockSpec((1, GDN_HEAD_DIM), const)],
        out_specs=pl.BlockSpec((c, GDN_VAL_WIDTH), lambda b, s: (b * nc + s, 0)),
        out_shape=jax.ShapeDtypeStruct((batch * seq, GDN_VAL_WIDTH), BF16),
        scratch_shapes=[pltpu.VMEM((c + HALO, GDN_CONV_CH), F32),
                        pltpu.VMEM((GDN_V_HEADS, GDN_HEAD_DIM, GDN_HEAD_DIM), F32)],
        compiler_params=_cparams(("arbitrary", "arbitrary")),
        name="gated_deltanet",
    )(proj_a, proj_a, proj_c, cw, pad(a_log), pad(dt_bias), norm_g.reshape(1, GDN_HEAD_DIM).astype(F32))


def _rope_kernel(pos_ref, freq_ref, sign_ref, c_ref, s_ref):
    tm = pos_ref.shape[0]
    ang = pos_ref[...].astype(F32) * freq_ref[...]
    c_ref[:, :MLA_NOPE] = jnp.ones((tm, MLA_NOPE), F32)
    c_ref[:, MLA_NOPE:] = jnp.cos(ang)
    s_ref[:, :MLA_NOPE] = jnp.zeros((tm, MLA_NOPE), F32)
    s_ref[:, MLA_NOPE:] = jnp.sin(ang) * sign_ref[...]


def rope_tables(positions, tm=512):
    t = positions.size
    inv_freq = ROPE_THETA ** (-jnp.arange(0, MLA_ROPE, 2, dtype=F32) / MLA_ROPE)
    width = MLA_QKP - MLA_NOPE
    pad = jnp.zeros((width - MLA_ROPE,), F32)
    freq = jnp.concatenate([inv_freq, inv_freq, pad]).reshape(1, width)
    half = MLA_ROPE // 2
    sign = jnp.concatenate([-jnp.ones((half,), F32), jnp.ones((half,), F32), pad]).reshape(1, width)
    return pl.pallas_call(
        _rope_kernel,
        grid=(t // tm,),
        in_specs=[pl.BlockSpec((tm, 1), lambda i: (i, 0)),
                  pl.BlockSpec((1, width), lambda i: (0, 0)),
                  pl.BlockSpec((1, width), lambda i: (0, 0))],
        out_specs=[pl.BlockSpec((tm, MLA_QKP), lambda i: (i, 0))] * 2,
        out_shape=[jax.ShapeDtypeStruct((t, MLA_QKP), F32)] * 2,
        compiler_params=_cparams(("arbitrary",)),
        name="rope_tables",
    )(positions.reshape(t, 1).astype(jnp.int32), freq, sign)


def _mla_prep_kernel(cq_ref, ckv_ref, kr_ref, krs_ref, cos_ref, sin_ref, qn_ref, kvn_ref,
                     wq_ref, wqs_ref, wkv_ref, q_out, k_out, v_out):
    def rms(x, g):
        return x * lax.rsqrt(jnp.mean(x * x, axis=-1, keepdims=True) + RMS_EPS) * g

    cqn = rms(cq_ref[...].astype(F32), qn_ref[...]).astype(BF16)
    ckvn = rms(ckv_ref[...].astype(F32), kvn_ref[...]).astype(BF16)
    cos = cos_ref[...]
    sin = sin_ref[...]
    kpe = (kr_ref[...] * cos[:, MLA_NOPE:] + krs_ref[...] * sin[:, MLA_NOPE:]).astype(k_out.dtype)
    q_all = _dot(cqn, wq_ref[...])
    qs_all = _dot(cqn, wqs_ref[...])
    kv_all = _dot(ckvn, wkv_ref[...])
    scale = (MLA_QK ** -0.5) * math.log2(math.e)
    for h in range(MLA_HEADS):
        cols = slice(h * MLA_QKP, (h + 1) * MLA_QKP)
        q_out[h] = ((q_all[:, cols] * cos + qs_all[:, cols] * sin) * scale).astype(q_out.dtype)
        base = h * (MLA_NOPE + MLA_V)
        k_out[h, :, :MLA_NOPE] = kv_all[:, base:base + MLA_NOPE].astype(k_out.dtype)
        k_out[h, :, MLA_NOPE:] = kpe
        v_out[h] = kv_all[:, base + MLA_NOPE:base + MLA_NOPE + MLA_V].astype(v_out.dtype)


def mla_prep(proj_a, proj_c, cos_t, sin_t, q_norm, kv_norm, w_uq, w_ukv, tm=512):
    t = proj_a.shape[0]
    hh = MLA_HEADS
    wq = w_uq.reshape(MLA_Q_RANK, hh, MLA_QK)
    half = MLA_ROPE // 2
    zeros = lambda n: jnp.zeros((MLA_Q_RANK, hh, n), w_uq.dtype)
    padw = lambda w: w.reshape(MLA_Q_RANK, hh * MLA_QKP).astype(BF16)
    wq_pad = padw(jnp.concatenate([wq, zeros(MLA_QKP - MLA_QK)], axis=-1))
    wq_swap = padw(jnp.concatenate([zeros(MLA_NOPE), wq[:, :, MLA_NOPE + half:],
                                    wq[:, :, MLA_NOPE:MLA_NOPE + half], zeros(MLA_QKP - MLA_QK)], axis=-1))
    row = lambda v: v.reshape(1, -1).astype(F32)
    const = lambda i: (0, 0)
    heads_tile = lambda i: (0, i, 0)
    return pl.pallas_call(
        _mla_prep_kernel,
        grid=(t // tm,),
        in_specs=[pl.BlockSpec((tm, MLA_Q_RANK), lambda i: (i, PA_CQ // MLA_Q_RANK)),
                  pl.BlockSpec((tm, MLA_KV_RANK), lambda i: (i, PA_CKV // MLA_KV_RANK)),
                  pl.BlockSpec((tm, LANES), lambda i: (i, 1)),
                  pl.BlockSpec((tm, LANES), lambda i: (i, 2)),
                  pl.BlockSpec((tm, MLA_QKP), lambda i: (i, 0)),
                  pl.BlockSpec((tm, MLA_QKP), lambda i: (i, 0)),
                  pl.BlockSpec((1, MLA_Q_RANK), const),
                  pl.BlockSpec((1, MLA_KV_RANK), const),
                  pl.BlockSpec((MLA_Q_RANK, hh * MLA_QKP), const),
                  pl.BlockSpec((MLA_Q_RANK, hh * MLA_QKP), const),
                  pl.BlockSpec((MLA_KV_RANK, hh * (MLA_NOPE + MLA_V)), const)],
        out_specs=[pl.BlockSpec((hh, tm, MLA_QKP), heads_tile),
                   pl.BlockSpec((hh, tm, MLA_QKP), heads_tile),
                   pl.BlockSpec((hh, tm, MLA_V), heads_tile)],
        out_shape=[jax.ShapeDtypeStruct((hh, t, MLA_QKP), BF16),
                   jax.ShapeDtypeStruct((hh, t, MLA_QKP), BF16),
                   jax.ShapeDtypeStruct((hh, t, MLA_V), BF16)],
        compiler_params=_cparams(("arbitrary",)),
        name="mla_prep",
    )(proj_a, proj_a, proj_c, proj_c, cos_t, sin_t, row(q_norm), row(kv_norm),
      wq_pad, wq_swap, w_ukv.astype(BF16))


def _flash_kernel(qi_tab, ki_tab, q_ref, k_ref, v_ref, o_ref, sa_ref, sb_ref, m_ref, acc_ref, *, tq, tk):
    p = pl.program_id(2)
    prev = jnp.maximum(p - 1, 0)
    qi = qi_tab[prev]
    ki = ki_tab[prev]
    last_k = (qi + 1) * (tq // tk) - 1
    even = p % 2 == 0
    needs_mask = (ki + 1) * tk > qi * tq + 1

    def update(masked, w_ref, r_ref):
        w_ref[...] = _dot_nt(q_ref[0], k_ref[0])
        sc = r_ref[...]
        if masked:
            row = qi * tq + lax.broadcasted_iota(jnp.int32, (tq, tk), 0)
            col = ki * tk + lax.broadcasted_iota(jnp.int32, (tq, tk), 1)
            sc = jnp.where(col <= row, sc, -jnp.inf)
        v_ext = jnp.concatenate([v_ref[0], jnp.ones((tk, LANES), BF16)], axis=1)
        m_old = m_ref[...]
        m_new = jnp.maximum(m_old, jnp.max(sc, axis=-1, keepdims=True))
        alpha = jnp.exp2(m_old - m_new)
        pr = jnp.exp2(sc - m_new).astype(BF16)
        acc_ref[...] = alpha * acc_ref[...] + _dot(pr, v_ext)
        m_ref[...] = m_new

    @pl.when(p == 0)
    def _():
        sa_ref[...] = _dot_nt(q_ref[0], k_ref[0])

    @pl.when(jnp.logical_and(p > 0, ki == 0))
    def _():
        m_ref[...] = jnp.full_like(m_ref, -jnp.inf)
        acc_ref[...] = jnp.zeros_like(acc_ref)

    for masked in (False, True):
        for is_even, w_ref, r_ref in ((True, sa_ref, sb_ref), (False, sb_ref, sa_ref)):
            cond = jnp.logical_and(p > 0, jnp.logical_and(needs_mask == masked, even == is_even))

            @pl.when(cond)
            def _(masked=masked, w_ref=w_ref, r_ref=r_ref):
                update(masked, w_ref, r_ref)

    @pl.when(jnp.logical_and(p > 0, ki == last_k))
    def _():
        acc = acc_ref[...]
        o_ref[...] = (acc[:, :MLA_V] / acc[:, MLA_V:MLA_V + 1]).astype(o_ref.dtype)


def flash_attention(q, k, v, *, batch, seq, tq=1024, tk=1024):
    nq = seq // tq
    nk = seq // tk
    pairs = [(a, b) for a in range(nq) for b in range((a + 1) * (tq // tk))]
    npairs = len(pairs)
    qi_tab = jnp.asarray([p[0] for p in pairs], jnp.int32)
    ki_tab = jnp.asarray([p[1] for p in pairs], jnp.int32)
    cur = lambda p: jnp.minimum(p, npairs - 1)
    prv = lambda p: jnp.maximum(p - 1, 0)
    scores = pltpu.VMEM((tq, tk), F32)
    grid_spec = pltpu.PrefetchScalarGridSpec(
        num_scalar_prefetch=2,
        grid=(batch, MLA_HEADS, npairs + 1),
        in_specs=[pl.BlockSpec((1, tq, MLA_QKP), lambda b, h, p, qt, kt: (h, b * nq + qt[cur(p)], 0)),
                  pl.BlockSpec((1, tk, MLA_QKP), lambda b, h, p, qt, kt: (h, b * nk + kt[cur(p)], 0)),
                  pl.BlockSpec((1, tk, MLA_V), lambda b, h, p, qt, kt: (h, b * nk + kt[prv(p)], 0))],
        out_specs=pl.BlockSpec((tq, MLA_V), lambda b, h, p, qt, kt: (b * nq + qt[prv(p)], h)),
        scratch_shapes=[scores, scores, pltpu.VMEM((tq, 1), F32), pltpu.VMEM((tq, MLA_V + LANES), F32)],
    )
    return pl.pallas_call(
        functools.partial(_flash_kernel, tq=tq, tk=tk),
        grid_spec=grid_spec,
        out_shape=jax.ShapeDtypeStruct((batch * seq, MLA_HEADS * MLA_V), BF16),
        compiler_params=_cparams(("arbitrary", "arbitrary", "arbitrary")),
        name="flash_attention",
    )(qi_tab, ki_tab, q, k, v)


def _merge_kernel(ha_ref, hb_ref, hc_ref, hd_ref, ga_ref, gb_ref, gc_ref, gd_ref, w_ref, o_ref):
    acc = None
    for b, (h_ref, g_ref) in enumerate(((ha_ref, ga_ref), (hb_ref, gb_ref), (hc_ref, gc_ref), (hd_ref, gd_ref))):
        term = g_ref[...].astype(F32) * _dot(h_ref[...], w_ref[b])
        acc = term if acc is None else acc + term
    o_ref[...] = acc.astype(o_ref.dtype)


def merge_branches(h_a, h_b, h_c, h_d, proj_a, w_branch, tm=1024, tn=512):
    t = h_a.shape[0]
    width = h_a.shape[1]
    g0 = PA_GATE // tn
    per = D_MODEL // tn
    hspec = pl.BlockSpec((tm, width), lambda i, j: (i, 0))
    gspecs = [pl.BlockSpec((tm, tn), functools.partial(lambda i, j, b: (i, g0 + b * per + j), b=b))
              for b in range(N_BRANCH)]
    return pl.pallas_call(
        _merge_kernel,
        grid=(t // tm, D_MODEL // tn),
        in_specs=[hspec] * 4 + gspecs + [pl.BlockSpec((N_BRANCH, width, tn), lambda i, j: (0, 0, j))],
        out_specs=pl.BlockSpec((tm, tn), lambda i, j: (i, j)),
        out_shape=jax.ShapeDtypeStruct((t, D_MODEL), BF16),
        compiler_params=_cparams(("arbitrary", "arbitrary")),
        name="merge_branches",
    )(h_a, h_b, h_c, h_d, proj_a, proj_a, proj_a, proj_a, w_branch)


def _ffn_up_kernel(x_ref, wg_ref, wu_ref, cwg_ref, cwu_ref, bg_ref, bu_ref, o_ref, gwin_ref, uwin_ref, *, tm):
    s = pl.program_id(2)

    @pl.when(s == 0)
    def _():
        gwin_ref[0:HALO, :] = jnp.zeros((HALO, gwin_ref.shape[1]), F32)
        uwin_ref[0:HALO, :] = jnp.zeros((HALO, uwin_ref.shape[1]), F32)

    x = x_ref[...]
    gwin_ref[HALO:HALO + tm, :] = _dot(x, wg_ref[...])
    uwin_ref[HALO:HALO + tm, :] = _dot(x, wu_ref[...])
    off = HALO - (FFN_CONV - 1)

    def conv(win_ref, cw_ref, b_ref):
        acc = b_ref[...] + win_ref[off:off + tm, :] * cw_ref[0:1, :]
        for j in range(1, FFN_CONV):
            acc = acc + win_ref[off + j:off + j + tm, :] * cw_ref[j:j + 1, :]
        return acc

    gate = conv(gwin_ref, cwg_ref, bg_ref)
    up = conv(uwin_ref, cwu_ref, bu_ref)
    o_ref[...] = (_silu(gate) * up).astype(o_ref.dtype)
    gwin_ref[0:HALO, :] = gwin_ref[tm:tm + HALO, :]
    uwin_ref[0:HALO, :] = uwin_ref[tm:tm + HALO, :]


def ffn_up(hn, w_up, conv_w, conv_b, *, batch, seq, tm=1024, tn=512):
    ns = seq // tm
    wg = w_up[:, :FFN_DIM].astype(BF16)
    wu = w_up[:, FFN_DIM:].astype(BF16)
    padw = lambda w: jnp.zeros((HALO, FFN_DIM), F32).at[:FFN_CONV].set(w.astype(F32))
    row = lambda v: v.reshape(1, FFN_DIM).astype(F32)
    wspec = pl.BlockSpec((D_MODEL, tn), lambda j, b, s: (0, j))
    cspec = pl.BlockSpec((HALO, tn), lambda j, b, s: (0, j))
    bspec = pl.BlockSpec((1, tn), lambda j, b, s: (0, j))
    return pl.pallas_call(
        functools.partial(_ffn_up_kernel, tm=tm),
        grid=(FFN_DIM // tn, batch, ns),
        in_specs=[pl.BlockSpec((tm, D_MODEL), lambda j, b, s: (b * ns + s, 0)),
                  wspec, wspec, cspec, cspec, bspec, bspec],
        out_specs=pl.BlockSpec((tm, tn), lambda j, b, s: (b * ns + s, j)),
        out_shape=jax.ShapeDtypeStruct((batch * seq, FFN_DIM), BF16),
        scratch_shapes=[pltpu.VMEM((tm + HALO, tn), F32), pltpu.VMEM((tm + HALO, tn), F32)],
        compiler_params=_cparams(("arbitrary", "arbitrary", "arbitrary")),
        name="ffn_up",
    )(hn, wg, wu, padw(conv_w[:, :FFN_DIM]), padw(conv_w[:, FFN_DIM:]),
      row(conv_b[:FFN_DIM]), row(conv_b[FFN_DIM:]))


def _in_proj_weights(w_in):
    o = np.cumsum([0, 1024, 512, 512, 1024, 1024, 8, 8, 2048, 512, 512, 64, 8192])
    col = lambda i: w_in[:, int(o[i]):int(o[i + 1])]
    pool, q, k, v, z, a, b, conf, cq, ckv, krope, gates = (col(i) for i in range(12))
    w_a = jnp.concatenate([q, k, v, conf, pool, z, cq, ckv, gates], axis=1).astype(BF16)
    half = MLA_ROPE // 2
    zeros = lambda n: jnp.zeros((D_MODEL, n), w_in.dtype)
    w_c = jnp.concatenate([a, b, zeros(112), krope, zeros(64),
                           krope[:, half:], krope[:, :half], zeros(64)], axis=1).astype(BF16)
    return w_a, w_c


def kernel(x, positions, mix_norm, w_in, pool_w, pool_scale, gdn_conv_w, gdn_a_log, gdn_dt_bias, gdn_norm, conf_conv_w, conf_conv_b, conf_ln_g, conf_ln_b, mla_q_norm, mla_w_uq, mla_kv_norm, mla_w_ukv, w_pool_out, w_gdn_out, w_conf_out, w_mla_out, w_out, ffn_norm, ffn_w_up, ffn_conv_w, ffn_conv_b, ffn_w_down, final_norm):
    batch, seq, d = x.shape
    t = batch * seq
    xf = x.reshape(t, d)
    cos_t, sin_t = rope_tables(positions)
    for l in range(DEPTH):
        w_a, w_c = _in_proj_weights(w_in[l])
        proj_a, proj_c = in_proj(xf, mix_norm[l], w_a, w_c, tm=1024, tn=1024, sigmoid_from=PA_GATE // 1024)
        h_a = pool_mixer(proj_a, pool_w[l], pool_scale[l], batch=batch, seq=seq)
        h_b = gated_deltanet(proj_a, proj_c, gdn_conv_w[l], gdn_a_log[l], gdn_dt_bias[l], gdn_norm[l],
                             batch=batch, seq=seq)
        h_c = conformer_conv(proj_a, conf_conv_w[l], conf_conv_b[l], conf_ln_g[l], conf_ln_b[l],
                             batch=batch, seq=seq)
        q, k, v = mla_prep(proj_a, proj_c, cos_t, sin_t, mla_q_norm[l], mla_kv_norm[l],
                           mla_w_uq[l], mla_w_ukv[l])
        h_d = flash_attention(q, k, v, batch=batch, seq=seq)
        w_branch = jnp.stack([w_pool_out[l], w_gdn_out[l], w_conf_out[l], w_mla_out[l]]).astype(BF16)
        merged = merge_branches(h_a, h_b, h_c, h_d, proj_a, w_branch)
        xf, hn = matmul(merged, w_out[l].astype(BF16), tm=512, tn=D_MODEL, tk=D_MODEL, out_dtype=F32,
                        residual=xf, norm_gain=ffn_norm[l], norm_dtype=BF16, name="out_proj")
        mid = ffn_up(hn, ffn_w_up[l], ffn_conv_w[l], ffn_conv_b[l], batch=batch, seq=seq)
        xf = matmul(mid, ffn_w_down[l].astype(BF16), tm=512, tn=1024, tk=FFN_DIM // 2, out_dtype=F32,
                    residual=xf, name="ffn_down")
    return rmsnorm(xf, final_norm, F32).reshape(batch, seq, d)
```

```python
import functools
import math

import numpy as np
import jax
import jax.numpy as jnp
from jax import lax
from jax.experimental import pallas as pl
from jax.experimental.pallas import tpu as pltpu

F32 = jnp.float32
BF16 = jnp.bfloat16

D_MODEL = 2048
DEPTH = 2
POOL_WINDOWS = (2, 4, 8, 16)
POOL_GROUPS = 4
POOL_GROUP_DIM = 256
POOL_WIDTH = 1024
GDN_K_HEADS = 4
GDN_V_HEADS = 8
GDN_HEAD_DIM = 128
GDN_KEY_WIDTH = 512
GDN_VAL_WIDTH = 1024
GDN_CONV_CH = 2048
GDN_CONV = 4
GDN_CHUNK = 64
CONF_WIDTH = 1024
CONF_CONV = 31
MLA_HEADS = 8
MLA_NOPE = 128
MLA_ROPE = 64
MLA_V = 128
MLA_QK = MLA_NOPE + MLA_ROPE
MLA_QKP = 256
MLA_Q_RANK = 512
MLA_KV_RANK = 512
ROPE_THETA = 10000.0
N_BRANCH = 4
FFN_DIM = 5632
FFN_CONV = 3
RMS_EPS = 1e-6
LN_EPS = 1e-5

VMEM_LIMIT_BYTES = 56 * 1024 * 1024
LANES = 128
HALO = 8
POOL_HALO = 16
CONF_HALO = 32

PA_QKV, PA_CONF, PA_POOL, PA_Z, PA_CQ, PA_CKV, PA_GATE = 0, 2048, 4096, 5120, 6144, 6656, 7168
PA_WIDTH = PA_GATE + N_BRANCH * D_MODEL
PC_WIDTH = 384


def _cparams(semantics):
    return pltpu.CompilerParams(dimension_semantics=semantics, vmem_limit_bytes=VMEM_LIMIT_BYTES)


def _sigmoid(x):
    return jax.nn.sigmoid(x)


def _silu(x):
    return x * jax.nn.sigmoid(x)


def _dot(a, b):
    return jnp.dot(a, b, preferred_element_type=F32)


def _dot_nt(a, b):
    return lax.dot_general(a, b, (((1,), (1,)), ((), ())), preferred_element_type=F32)


def _rmsnorm_kernel(x_ref, g_ref, o_ref):
    x = x_ref[...]
    ms = jnp.mean(x * x, axis=-1, keepdims=True)
    o_ref[...] = (x * lax.rsqrt(ms + RMS_EPS) * g_ref[...]).astype(o_ref.dtype)


def rmsnorm(x, g, out_dtype, tm=512):
    m, d = x.shape
    return pl.pallas_call(
        _rmsnorm_kernel,
        grid=(m // tm,),
        in_specs=[pl.BlockSpec((tm, d), lambda i: (i, 0)), pl.BlockSpec((1, d), lambda i: (0, 0))],
        out_specs=pl.BlockSpec((tm, d), lambda i: (i, 0)),
        out_shape=jax.ShapeDtypeStruct((m, d), out_dtype),
        compiler_params=_cparams(("arbitrary",)),
        name="rmsnorm",
    )(x, g.reshape(1, d).astype(F32))


def _mm_kernel(*refs, nk, has_res, has_norm):
    refs = list(refs)
    a_ref, w_ref = refs[:2]
    pos = 2
    r_ref = g_ref = None
    if has_res:
        r_ref = refs[pos]
        pos += 1
    if has_norm:
        g_ref = refs[pos]
        pos += 1
    outs = refs[pos:-1]
    acc_ref = refs[-1]
    k = pl.program_id(2)
    part = _dot(a_ref[...], w_ref[...])

    def finish(acc):
        if r_ref is not None:
            acc = acc + r_ref[...]
        outs[0][...] = acc.astype(outs[0].dtype)
        if has_norm:
            y = acc * lax.rsqrt(jnp.mean(acc * acc, axis=-1, keepdims=True) + RMS_EPS) * g_ref[...]
            outs[1][...] = y.astype(outs[1].dtype)

    if nk == 1:
        finish(part)
    else:
        @pl.when(k == 0)
        def _():
            acc_ref[...] = part

        @pl.when(jnp.logical_and(k > 0, k < nk - 1))
        def _():
            acc_ref[...] += part

        @pl.when(k == nk - 1)
        def _():
            finish(acc_ref[...] + part)


def matmul(a, w, *, tm, tn, tk, out_dtype, residual=None, norm_gain=None, norm_dtype=None, name="matmul"):
    m, kdim = a.shape
    n = w.shape[1]
    nk = kdim // tk
    in_specs = [pl.BlockSpec((tm, tk), lambda i, j, k: (i, k)),
                pl.BlockSpec((tk, tn), lambda i, j, k: (k, j))]
    args = [a, w]
    if residual is not None:
        in_specs.append(pl.BlockSpec((tm, tn), lambda i, j, k: (i, j)))
        args.append(residual)
    out_spec = pl.BlockSpec((tm, tn), lambda i, j, k: (i, j))
    out_specs, out_shape = out_spec, jax.ShapeDtypeStruct((m, n), out_dtype)
    if norm_gain is not None:
        assert tn == n
        in_specs.append(pl.BlockSpec((1, n), lambda i, j, k: (0, 0)))
        args.append(norm_gain.reshape(1, n).astype(F32))
        out_specs = [out_spec, out_spec]
        out_shape = [out_shape, jax.ShapeDtypeStruct((m, n), norm_dtype)]
    return pl.pallas_call(
        functools.partial(_mm_kernel, nk=nk, has_res=residual is not None, has_norm=norm_gain is not None),
        grid=(m // tm, n // tn, nk),
        in_specs=in_specs,
        out_specs=out_specs,
        out_shape=out_shape,
        scratch_shapes=[pltpu.VMEM((tm, tn) if nk > 1 else (HALO, LANES), F32)],
        compiler_params=_cparams(("arbitrary", "arbitrary", "arbitrary")),
        name=name,
    )(*args)


def _in_proj_kernel(x_ref, g_ref, w_ref, wc_ref, o_ref, oc_ref, xn_ref):
    j = pl.program_id(1)

    @pl.when(j == 0)
    def _():
        x = x_ref[...]
        xn = (x * lax.rsqrt(jnp.mean(x * x, axis=-1, keepdims=True) + RMS_EPS) * g_ref[...]).astype(BF16)
        xn_ref[...] = xn
        oc_ref[...] = _dot(xn, wc_ref[...])

    o_ref[...] = _dot(xn_ref[...], w_ref[...]).astype(o_ref.dtype)


def in_proj(x, gain, w, w_small, *, tm, tn):
    m, d = x.shape
    n = w.shape[1]
    ns = w_small.shape[1]
    return pl.pallas_call(
        _in_proj_kernel,
        grid=(m // tm, n // tn),
        in_specs=[pl.BlockSpec((tm, d), lambda i, j: (i, 0)),
                  pl.BlockSpec((1, d), lambda i, j: (0, 0)),
                  pl.BlockSpec((d, tn), lambda i, j: (0, j)),
                  pl.BlockSpec((d, ns), lambda i, j: (0, 0))],
        out_specs=[pl.BlockSpec((tm, tn), lambda i, j: (i, j)),
                   pl.BlockSpec((tm, ns), lambda i, j: (i, 0))],
        out_shape=[jax.ShapeDtypeStruct((m, n), BF16), jax.ShapeDtypeStruct((m, ns), F32)],
        scratch_shapes=[pltpu.VMEM((tm, d), BF16)],
        compiler_params=_cparams(("arbitrary", "arbitrary")),
        name="in_proj",
    )(x, gain.reshape(1, d).astype(F32), w, w_small)


def _pool_bands(ts):
    i = np.arange(ts)[:, None]
    m = np.arange(ts + POOL_HALO)[None, :]
    lag = (i + POOL_HALO) - m
    return np.stack([((lag >= 0) & (lag < w)) for w in POOL_WINDOWS]).astype(np.float32)


def _pool_kernel(u_ref, band_ref, pw_ref, scale_ref, o_ref, tail_ref, *, ts):
    s = pl.program_id(1)

    @pl.when(s == 0)
    def _():
        tail_ref[...] = jnp.zeros_like(tail_ref)

    t_abs = s * ts + lax.broadcasted_iota(jnp.int32, (ts, 1), 0)
    for g, win_len in enumerate(POOL_WINDOWS):
        cols = slice(g * POOL_GROUP_DIM, (g + 1) * POOL_GROUP_DIM)
        ug = u_ref[:, cols]
        win = jnp.concatenate([tail_ref[:, cols], ug], axis=0)
        sums = _dot(band_ref[g], win)
        cnt = jnp.minimum(t_abs + 1, win_len).astype(F32)
        diff = sums / cnt - ug.astype(F32)
        y = _dot(diff.astype(BF16), pw_ref[g]) * scale_ref[:, cols]
        o_ref[:, cols] = y.astype(o_ref.dtype)
    tail_ref[...] = u_ref[ts - POOL_HALO:, :]


def pool_mixer(proj_a, pool_w, pool_scale, *, batch, seq, ts=512):
    ns = seq // ts
    bands = jnp.asarray(_pool_bands(ts), BF16)
    cb = PA_POOL // POOL_WIDTH
    return pl.pallas_call(
        functools.partial(_pool_kernel, ts=ts),
        grid=(batch, ns),
        in_specs=[pl.BlockSpec((ts, POOL_WIDTH), lambda b, s: (b * ns + s, cb)),
                  pl.BlockSpec((POOL_GROUPS, ts, ts + POOL_HALO), lambda b, s: (0, 0, 0)),
                  pl.BlockSpec((POOL_GROUPS, POOL_GROUP_DIM, POOL_GROUP_DIM), lambda b, s: (0, 0, 0)),
                  pl.BlockSpec((1, POOL_WIDTH), lambda b, s: (0, 0))],
        out_specs=pl.BlockSpec((ts, POOL_WIDTH), lambda b, s: (b * ns + s, 0)),
        out_shape=jax.ShapeDtypeStruct((batch * seq, POOL_WIDTH), BF16),
        scratch_shapes=[pltpu.VMEM((POOL_HALO, POOL_WIDTH), BF16)],
        compiler_params=_cparams(("arbitrary", "arbitrary")),
        name="pool_mixer",
    )(proj_a, bands, pool_w.astype(BF16), pool_scale.reshape(1, POOL_WIDTH).astype(F32))


def _conf_kernel(u_ref, cw_ref, cb_ref, lg_ref, lb_ref, o_ref, win_ref, sh_ref, y_ref, *, ts, rc, rn):
    s = pl.program_id(1)
    span = ts + CONF_HALO - HALO

    @pl.when(s == 0)
    def _():
        win_ref[0:CONF_HALO, :] = jnp.zeros((CONF_HALO, CONF_WIDTH), F32)

    a = u_ref[:, :CONF_WIDTH].astype(F32)
    gate = u_ref[:, CONF_WIDTH:].astype(F32)
    win_ref[CONF_HALO:CONF_HALO + ts, :] = a * _sigmoid(gate)
    for r in range(1, HALO):
        sh_ref[r - 1, 0:span, :] = win_ref[r:r + span, :]
    off = CONF_HALO - (CONF_CONV - 1)

    for base in range(0, ts, rc):
        for c in range(CONF_WIDTH // LANES):
            lanes = slice(c * LANES, (c + 1) * LANES)
            acc = None
            for j in range(CONF_CONV):
                start = base + ((off + j) // HALO) * HALO
                r = (off + j) % HALO
                x = win_ref[start:start + rc, lanes] if r == 0 else sh_ref[r - 1, start:start + rc, lanes]
                term = x * cw_ref[j:j + 1, lanes]
                acc = term if acc is None else acc + term
            y_ref[base:base + rc, lanes] = acc + cb_ref[:, lanes]

    def norm_rows(i, carry):
        rows = pl.ds(pl.multiple_of(i * rn, rn), rn)
        h = y_ref[rows, :]
        mu = jnp.mean(h, axis=-1, keepdims=True)
        hc = h - mu
        var = jnp.mean(hc * hc, axis=-1, keepdims=True)
        y = hc * lax.rsqrt(var + LN_EPS) * lg_ref[...] + lb_ref[...]
        o_ref[rows, :] = _silu(y).astype(o_ref.dtype)
        return carry

    lax.fori_loop(0, ts // rn, norm_rows, 0, unroll=4)
    win_ref[0:CONF_HALO, :] = win_ref[ts:ts + CONF_HALO, :]


def conformer_conv(proj_a, conv_w, conv_b, ln_g, ln_b, *, batch, seq, ts=256, rc=128, rn=32):
    ns = seq // ts
    cw = jnp.zeros((CONF_HALO, CONF_WIDTH), F32).at[:CONF_CONV].set(conv_w.astype(F32))
    row = lambda v: v.reshape(1, CONF_WIDTH).astype(F32)
    cb = PA_CONF // (2 * CONF_WIDTH)
    const = lambda b, s: (0, 0)
    return pl.pallas_call(
        functools.partial(_conf_kernel, ts=ts, rc=rc, rn=rn),
        grid=(batch, ns),
        in_specs=[pl.BlockSpec((ts, 2 * CONF_WIDTH), lambda b, s: (b * ns + s, cb)),
                  pl.BlockSpec((CONF_HALO, CONF_WIDTH), const),
                  pl.BlockSpec((1, CONF_WIDTH), const),
                  pl.BlockSpec((1, CONF_WIDTH), const),
                  pl.BlockSpec((1, CONF_WIDTH), const)],
        out_specs=pl.BlockSpec((ts, CONF_WIDTH), lambda b, s: (b * ns + s, 0)),
        out_shape=jax.ShapeDtypeStruct((batch * seq, CONF_WIDTH), BF16),
        scratch_shapes=[pltpu.VMEM((ts + CONF_HALO, CONF_WIDTH), F32),
                        pltpu.VMEM((HALO - 1, ts + CONF_HALO - HALO, CONF_WIDTH), F32),
                        pltpu.VMEM((ts, CONF_WIDTH), F32)],
        compiler_params=_cparams(("arbitrary", "arbitrary")),
        name="conformer_conv",
    )(proj_a, cw, row(conv_b), row(ln_g), row(ln_b))


def _split(a):
    hi = a.astype(BF16)
    return hi, (a - hi.astype(F32)).astype(BF16)


def _split_dots(lhs_list, rhs):
    r_hi, r_lo = _split(rhs)
    parts = [_split(l) for l in lhs_list]
    his = [p[0] for p in parts]
    los = [p[1] for p in parts]
    n = len(lhs_list)
    m = lhs_list[0].shape[0]
    top = _dot(jnp.concatenate(his + los, axis=0), r_hi)
    bot = _dot(jnp.concatenate(his, axis=0) if n > 1 else his[0], r_lo)
    return [top[i * m:(i + 1) * m] + (top[(n + i) * m:(n + i + 1) * m] + bot[i * m:(i + 1) * m])
            for i in range(n)]


def _gdn_kernel(qkv_ref, z_ref, ab_ref, cw_ref, alog_ref, dtb_ref, ng_ref, o_ref, win_ref, state_ref, *, nck):
    c = GDN_CHUNK
    dh = GDN_HEAD_DIM
    rows = nck * c
    step = pl.program_id(1)

    @pl.when(step == 0)
    def _():
        win_ref[0:HALO, :] = jnp.zeros((HALO, GDN_CONV_CH), F32)
        state_ref[...] = jnp.zeros_like(state_ref)

    win_ref[HALO:HALO + rows, :] = qkv_ref[...].astype(F32)
    off = HALO - (GDN_CONV - 1)
    y = jnp.zeros((rows, GDN_CONV_CH), F32)
    for j in range(GDN_CONV):
        y = y + win_ref[off + j:off + j + rows, :] * cw_ref[j:j + 1, :]
    y = _silu(y)
    win_ref[0:HALO, :] = win_ref[rows:rows + HALO, :]

    ab = ab_ref[...]
    g_all = -jnp.exp(alog_ref[...]) * jnp.logaddexp(ab + dtb_ref[...], 0.0)
    beta_all = _sigmoid(ab)
    ii = lax.broadcasted_iota(jnp.int32, (c, c), 0)
    jj = lax.broadcasted_iota(jnp.int32, (c, c), 1)
    lower = ii >= jj
    strict = ii > jj
    tri = lower.astype(F32)
    eye = (ii == jj).astype(F32)

    def l2n(t):
        return t * lax.rsqrt(jnp.sum(t * t, axis=-1, keepdims=True) + 1e-6)

    qn = [l2n(y[:, kh * dh:(kh + 1) * dh]) * (dh ** -0.5) for kh in range(GDN_K_HEADS)]
    kn = [l2n(y[:, GDN_KEY_WIDTH + kh * dh:GDN_KEY_WIDTH + (kh + 1) * dh]) for kh in range(GDN_K_HEADS)]

    rep = GDN_V_HEADS // GDN_K_HEADS
    cks = range(nck)
    heads = range(GDN_V_HEADS)
    units = [(ck, h) for ck in cks for h in heads]
    rs = lambda ck: slice(ck * c, (ck + 1) * c)
    gc_all = [jnp.dot(tri, g_all[rs(ck)], precision=lax.Precision.HIGHEST, preferred_element_type=F32)
              for ck in cks]
    gc_t = [g.T for g in gc_all]
    q_c = {(ck, kh): qn[kh][rs(ck)] for ck in cks for kh in range(GDN_K_HEADS)}
    k_c = {(ck, kh): kn[kh][rs(ck)] for ck in cks for kh in range(GDN_K_HEADS)}
    kt_c = {key: val.T.astype(BF16) for key, val in k_c.items()}
    gcol = {(ck, h): gc_all[ck][:, h:h + 1] for ck, h in units}
    beta = {(ck, h): beta_all[rs(ck), 8 + h:9 + h] for ck, h in units}
    decay = {u: jnp.where(lower, jnp.exp(jnp.where(lower, gcol[u] - gc_t[u[0]][u[1]:u[1] + 1, :], 0.0)), 0.0)
             for u in units}
    kb = {(ck, h): k_c[ck, h // rep] * beta[ck, h] for ck, h in units}
    vb = {(ck, h): y[rs(ck), 2 * GDN_KEY_WIDTH + h * dh:2 * GDN_KEY_WIDTH + (h + 1) * dh] * beta[ck, h]
          for ck, h in units}
    kq = {(ck, kh): _dot(jnp.concatenate([kb[ck, kh * rep + r].astype(BF16) for r in range(rep)]
                                         + [q_c[ck, kh].astype(BF16)], axis=0), kt_c[ck, kh])
          for ck in cks for kh in range(GDN_K_HEADS)}
    lmat = {(ck, h): jnp.where(strict, kq[ck, h // rep][(h % rep) * c:(h % rep + 1) * c] * decay[ck, h], 0.0)
            for ck, h in units}
    qk = {(ck, h): kq[ck, h // rep][rep * c:(rep + 1) * c] * decay[ck, h] for ck, h in units}
    tinv = {u: eye - lmat[u] for u in units}
    pw = {u: _split_dots([lmat[u]], lmat[u])[0] for u in units}
    for _ in range(4):
        both = {u: _split_dots([pw[u], tinv[u]], pw[u]) for u in units}
        pw = {u: both[u][0] for u in units}
        tinv = {u: tinv[u] + both[u][1] for u in units}
    tinv = {u: tinv[u] + _split_dots([tinv[u]], pw[u])[0] for u in units}
    eg = {u: jnp.exp(gcol[u]) for u in units}
    uw = {u: _split_dots([tinv[u]], jnp.concatenate([vb[u], kb[u] * eg[u]], axis=1))[0] for u in units}
    qe = {(ck, h): (q_c[ck, h // rep] * eg[ck, h]).astype(BF16) for ck, h in units}
    glast = {u: gcol[u][c - 1:c, :] for u in units}
    dlast = {u: jnp.exp(glast[u] - gcol[u]) for u in units}

    state = [state_ref[h] for h in heads]
    for ck in cks:
        ws = [_dot(jnp.concatenate([uw[ck, h][:, dh:].astype(BF16), qe[ck, h]], axis=0),
                   state[h].astype(BF16)) for h in heads]
        v_new = [uw[ck, h][:, :dh] - ws[h][:c] for h in heads]
        o = [ws[h][c:] + _dot(qk[ck, h].astype(BF16), v_new[h].astype(BF16)) for h in heads]
        state = [state[h] * jnp.exp(glast[ck, h])
                 + _dot(kt_c[ck, h // rep], (v_new[h] * dlast[ck, h]).astype(BF16)) for h in heads]
        for h in heads:
            zh = z_ref[rs(ck), h * dh:(h + 1) * dh].astype(F32)
            on = o[h] * lax.rsqrt(jnp.mean(o[h] * o[h], axis=-1, keepdims=True) + RMS_EPS) * ng_ref[...]
            o_ref[rs(ck), h * dh:(h + 1) * dh] = (on * _silu(zh)).astype(o_ref.dtype)
    for h in heads:
        state_ref[h] = state[h]


def gated_deltanet(proj_a, proj_c, conv_w, a_log, dt_bias, norm_g, *, batch, seq, nck=2):
    rows = nck * GDN_CHUNK
    nc = seq // rows
    cw = jnp.zeros((HALO, GDN_CONV_CH), F32).at[:GDN_CONV].set(conv_w.astype(F32))
    pad = lambda v: jnp.zeros((1, LANES), F32).at[0, :GDN_V_HEADS].set(v.astype(F32))
    const = lambda b, s: (0, 0)
    return pl.pallas_call(
        functools.partial(_gdn_kernel, nck=nck),
        grid=(batch, nc),
        in_specs=[pl.BlockSpec((rows, GDN_CONV_CH), lambda b, s: (b * nc + s, PA_QKV // GDN_CONV_CH)),
                  pl.BlockSpec((rows, GDN_VAL_WIDTH), lambda b, s: (b * nc + s, PA_Z // GDN_VAL_WIDTH)),
                  pl.BlockSpec((rows, LANES), lambda b, s: (b * nc + s, 0)),
                  pl.BlockSpec((HALO, GDN_CONV_CH), const),
                  pl.BlockSpec((1, LANES), const),
                  pl.BlockSpec((1, LANES), const),
                  pl.BlockSpec((1, GDN_HEAD_DIM), const)],
        out_specs=pl.BlockSpec((rows, GDN_VAL_WIDTH), lambda b, s: (b * nc + s, 0)),
        out_shape=jax.ShapeDtypeStruct((batch * seq, GDN_VAL_WIDTH), BF16),
        scratch_shapes=[pltpu.VMEM((rows + HALO, GDN_CONV_CH), F32),
                        pltpu.VMEM((GDN_V_HEADS, GDN_HEAD_DIM, GDN_HEAD_DIM), F32)],
        compiler_params=_cparams(("arbitrary", "arbitrary")),
        name="gated_deltanet",
    )(proj_a, proj_a, proj_c, cw, pad(a_log), pad(dt_bias), norm_g.reshape(1, GDN_HEAD_DIM).astype(F32))


def _rope_kernel(pos_ref, freq_ref, sign_ref, c_ref, s_ref):
    tm = pos_ref.shape[0]
    ang = pos_ref[...].astype(F32) * freq_ref[...]
    c_ref[:, :MLA_NOPE] = jnp.ones((tm, MLA_NOPE), F32)
    c_ref[:, MLA_NOPE:] = jnp.cos(ang)
    s_ref[:, :MLA_NOPE] = jnp.zeros((tm, MLA_NOPE), F32)
    s_ref[:, MLA_NOPE:] = jnp.sin(ang) * sign_ref[...]


def rope_tables(positions, tm=512):
    t = positions.size
    inv_freq = ROPE_THETA ** (-jnp.arange(0, MLA_ROPE, 2, dtype=F32) / MLA_ROPE)
    width = MLA_QKP - MLA_NOPE
    pad = jnp.zeros((width - MLA_ROPE,), F32)
    freq = jnp.concatenate([inv_freq, inv_freq, pad]).reshape(1, width)
    half = MLA_ROPE // 2
    sign = jnp.concatenate([-jnp.ones((half,), F32), jnp.ones((half,), F32), pad]).reshape(1, width)
    return pl.pallas_call(
        _rope_kernel,
        grid=(t // tm,),
        in_specs=[pl.BlockSpec((tm, 1), lambda i: (i, 0)),
                  pl.BlockSpec((1, width), lambda i: (0, 0)),
                  pl.BlockSpec((1, width), lambda i: (0, 0))],
        out_specs=[pl.BlockSpec((tm, MLA_QKP), lambda i: (i, 0))] * 2,
        out_shape=[jax.ShapeDtypeStruct((t, MLA_QKP), F32)] * 2,
        compiler_params=_cparams(("arbitrary",)),
        name="rope_tables",
    )(positions.reshape(t, 1).astype(jnp.int32), freq, sign)


def _mla_prep_kernel(cq_ref, ckv_ref, kr_ref, krs_ref, cos_ref, sin_ref, qn_ref, kvn_ref,
                     wq_ref, wqs_ref, wkv_ref, q_out, k_out, v_out):
    def rms(x, g):
        return x * lax.rsqrt(jnp.mean(x * x, axis=-1, keepdims=True) + RMS_EPS) * g

    cqn = rms(cq_ref[...].astype(F32), qn_ref[...]).astype(BF16)
    ckvn = rms(ckv_ref[...].astype(F32), kvn_ref[...]).astype(BF16)
    cos = cos_ref[...]
    sin = sin_ref[...]
    kpe = (kr_ref[...] * cos[:, MLA_NOPE:] + krs_ref[...] * sin[:, MLA_NOPE:]).astype(k_out.dtype)
    q_all = _dot(cqn, wq_ref[...])
    qs_all = _dot(cqn, wqs_ref[...])
    kv_all = _dot(ckvn, wkv_ref[...])
    scale = (MLA_QK ** -0.5) * math.log2(math.e)
    for h in range(MLA_HEADS):
        cols = slice(h * MLA_QKP, (h + 1) * MLA_QKP)
        q_out[h] = ((q_all[:, cols] * cos + qs_all[:, cols] * sin) * scale).astype(q_out.dtype)
        base = h * (MLA_NOPE + MLA_V)
        k_out[h, :, :MLA_NOPE] = kv_all[:, base:base + MLA_NOPE].astype(k_out.dtype)
        k_out[h, :, MLA_NOPE:] = kpe
        v_out[h] = kv_all[:, base + MLA_NOPE:base + MLA_NOPE + MLA_V].astype(v_out.dtype)


def mla_prep(proj_a, proj_c, cos_t, sin_t, q_norm, kv_norm, w_uq, w_ukv, tm=512):
    t = proj_a.shape[0]
    hh = MLA_HEADS
    wq = w_uq.reshape(MLA_Q_RANK, hh, MLA_QK)
    half = MLA_ROPE // 2
    zeros = lambda n: jnp.zeros((MLA_Q_RANK, hh, n), w_uq.dtype)
    padw = lambda w: w.reshape(MLA_Q_RANK, hh * MLA_QKP).astype(BF16)
    wq_pad = padw(jnp.concatenate([wq, zeros(MLA_QKP - MLA_QK)], axis=-1))
    wq_swap = padw(jnp.concatenate([zeros(MLA_NOPE), wq[:, :, MLA_NOPE + half:],
                                    wq[:, :, MLA_NOPE:MLA_NOPE + half], zeros(MLA_QKP - MLA_QK)], axis=-1))
    row = lambda v: v.reshape(1, -1).astype(F32)
    const = lambda i: (0, 0)
    heads_tile = lambda i: (0, i, 0)
    return pl.pallas_call(
        _mla_prep_kernel,
        grid=(t // tm,),
        in_specs=[pl.BlockSpec((tm, MLA_Q_RANK), lambda i: (i, PA_CQ // MLA_Q_RANK)),
                  pl.BlockSpec((tm, MLA_KV_RANK), lambda i: (i, PA_CKV // MLA_KV_RANK)),
                  pl.BlockSpec((tm, LANES), lambda i: (i, 1)),
                  pl.BlockSpec((tm, LANES), lambda i: (i, 2)),
                  pl.BlockSpec((tm, MLA_QKP), lambda i: (i, 0)),
                  pl.BlockSpec((tm, MLA_QKP), lambda i: (i, 0)),
                  pl.BlockSpec((1, MLA_Q_RANK), const),
                  pl.BlockSpec((1, MLA_KV_RANK), const),
                  pl.BlockSpec((MLA_Q_RANK, hh * MLA_QKP), const),
                  pl.BlockSpec((MLA_Q_RANK, hh * MLA_QKP), const),
                  pl.BlockSpec((MLA_KV_RANK, hh * (MLA_NOPE + MLA_V)), const)],
        out_specs=[pl.BlockSpec((hh, tm, MLA_QKP), heads_tile),
                   pl.BlockSpec((hh, tm, MLA_QKP), heads_tile),
                   pl.BlockSpec((hh, tm, MLA_V), heads_tile)],
        out_shape=[jax.ShapeDtypeStruct((hh, t, MLA_QKP), BF16),
                   jax.ShapeDtypeStruct((hh, t, MLA_QKP), BF16),
                   jax.ShapeDtypeStruct((hh, t, MLA_V), BF16)],
        compiler_params=_cparams(("arbitrary",)),
        name="mla_prep",
    )(proj_a, proj_a, proj_c, proj_c, cos_t, sin_t, row(q_norm), row(kv_norm),
      wq_pad, wq_swap, w_ukv.astype(BF16))


def _flash_kernel(qi_tab, ki_tab, q_ref, k_ref, v_ref, o_ref, sa_ref, sb_ref, m_ref, acc_ref, *, tq, tk):
    p = pl.program_id(2)
    prev = jnp.maximum(p - 1, 0)
    qi = qi_tab[prev]
    ki = ki_tab[prev]
    last_k = (qi + 1) * (tq // tk) - 1
    even = p % 2 == 0
    needs_mask = (ki + 1) * tk > qi * tq + 1

    def update(masked, w_ref, r_ref):
        w_ref[...] = _dot_nt(q_ref[0], k_ref[0])
        sc = r_ref[...]
        if masked:
            row = qi * tq + lax.broadcasted_iota(jnp.int32, (tq, tk), 0)
            col = ki * tk + lax.broadcasted_iota(jnp.int32, (tq, tk), 1)
            sc = jnp.where(col <= row, sc, -jnp.inf)
        v_ext = jnp.concatenate([v_ref[0], jnp.ones((tk, LANES), BF16)], axis=1)
        m_old = m_ref[...]
        m_new = jnp.maximum(m_old, jnp.max(sc, axis=-1, keepdims=True))
        alpha = jnp.exp2(m_old - m_new)
        pr = jnp.exp2(sc - m_new).astype(BF16)
        acc_ref[...] = alpha * acc_ref[...] + _dot(pr, v_ext)
        m_ref[...] = m_new

    @pl.when(p == 0)
    def _():
        sa_ref[...] = _dot_nt(q_ref[0], k_ref[0])

    @pl.when(jnp.logical_and(p > 0, ki == 0))
    def _():
        m_ref[...] = jnp.full_like(m_ref, -jnp.inf)
        acc_ref[...] = jnp.zeros_like(acc_ref)

    for masked in (False, True):
        for is_even, w_ref, r_ref in ((True, sa_ref, sb_ref), (False, sb_ref, sa_ref)):
            cond = jnp.logical_and(p > 0, jnp.logical_and(needs_mask == masked, even == is_even))

            @pl.when(cond)
            def _(masked=masked, w_ref=w_ref, r_ref=r_ref):
                update(masked, w_ref, r_ref)

    @pl.when(jnp.logical_and(p > 0, ki == last_k))
    def _():
        acc = acc_ref[...]
        o_ref[...] = (acc[:, :MLA_V] / acc[:, MLA_V:MLA_V + 1]).astype(o_ref.dtype)


def flash_attention(q, k, v, *, batch, seq, tq=1024, tk=1024):
    nq = seq // tq
    nk = seq // tk
    pairs = [(a, b) for a in range(nq) for b in range((a + 1) * (tq // tk))]
    npairs = len(pairs)
    qi_tab = jnp.asarray([p[0] for p in pairs], jnp.int32)
    ki_tab = jnp.asarray([p[1] for p in pairs], jnp.int32)
    cur = lambda p: jnp.minimum(p, npairs - 1)
    prv = lambda p: jnp.maximum(p - 1, 0)
    scores = pltpu.VMEM((tq, tk), F32)
    grid_spec = pltpu.PrefetchScalarGridSpec(
        num_scalar_prefetch=2,
        grid=(batch, MLA_HEADS, npairs + 1),
        in_specs=[pl.BlockSpec((1, tq, MLA_QKP), lambda b, h, p, qt, kt: (h, b * nq + qt[cur(p)], 0)),
                  pl.BlockSpec((1, tk, MLA_QKP), lambda b, h, p, qt, kt: (h, b * nk + kt[cur(p)], 0)),
                  pl.BlockSpec((1, tk, MLA_V), lambda b, h, p, qt, kt: (h, b * nk + kt[prv(p)], 0))],
        out_specs=pl.BlockSpec((tq, MLA_V), lambda b, h, p, qt, kt: (b * nq + qt[prv(p)], h)),
        scratch_shapes=[scores, scores, pltpu.VMEM((tq, 1), F32), pltpu.VMEM((tq, MLA_V + LANES), F32)],
    )
    return pl.pallas_call(
        functools.partial(_flash_kernel, tq=tq, tk=tk),
        grid_spec=grid_spec,
        out_shape=jax.ShapeDtypeStruct((batch * seq, MLA_HEADS * MLA_V), BF16),
        compiler_params=_cparams(("arbitrary", "arbitrary", "arbitrary")),
        name="flash_attention",
    )(qi_tab, ki_tab, q, k, v)


def _merge_kernel(ha_ref, hb_ref, hc_ref, hd_ref, ga_ref, gb_ref, gc_ref, gd_ref, w_ref, o_ref):
    acc = None
    for b, (h_ref, g_ref) in enumerate(((ha_ref, ga_ref), (hb_ref, gb_ref), (hc_ref, gc_ref), (hd_ref, gd_ref))):
        gate = 0.5 * jnp.tanh(0.5 * g_ref[...].astype(F32)) + 0.5
        term = gate * _dot(h_ref[...], w_ref[b])
        acc = term if acc is None else acc + term
    o_ref[...] = acc.astype(o_ref.dtype)


def merge_branches(h_a, h_b, h_c, h_d, proj_a, w_branch, tm=1024, tn=512):
    t = h_a.shape[0]
    width = h_a.shape[1]
    g0 = PA_GATE // tn
    per = D_MODEL // tn
    hspec = pl.BlockSpec((tm, width), lambda i, j: (i, 0))
    gspecs = [pl.BlockSpec((tm, tn), functools.partial(lambda i, j, b: (i, g0 + b * per + j), b=b))
              for b in range(N_BRANCH)]
    return pl.pallas_call(
        _merge_kernel,
        grid=(t // tm, D_MODEL // tn),
        in_specs=[hspec] * 4 + gspecs + [pl.BlockSpec((N_BRANCH, width, tn), lambda i, j: (0, 0, j))],
        out_specs=pl.BlockSpec((tm, tn), lambda i, j: (i, j)),
        out_shape=jax.ShapeDtypeStruct((t, D_MODEL), BF16),
        compiler_params=_cparams(("arbitrary", "arbitrary")),
        name="merge_branches",
    )(h_a, h_b, h_c, h_d, proj_a, proj_a, proj_a, proj_a, w_branch)


def _ffn_up_kernel(x_ref, wg_ref, wu_ref, cwg_ref, cwu_ref, bg_ref, bu_ref, o_ref, gwin_ref, uwin_ref, *, tm):
    s = pl.program_id(2)

    @pl.when(s == 0)
    def _():
        gwin_ref[0:HALO, :] = jnp.zeros((HALO, gwin_ref.shape[1]), F32)
        uwin_ref[0:HALO, :] = jnp.zeros((HALO, uwin_ref.shape[1]), F32)

    x = x_ref[...]
    gwin_ref[HALO:HALO + tm, :] = _dot(x, wg_ref[...])
    uwin_ref[HALO:HALO + tm, :] = _dot(x, wu_ref[...])
    off = HALO - (FFN_CONV - 1)

    def conv(win_ref, cw_ref, b_ref):
        acc = b_ref[...] + win_ref[off:off + tm, :] * cw_ref[0:1, :]
        for j in range(1, FFN_CONV):
            acc = acc + win_ref[off + j:off + j + tm, :] * cw_ref[j:j + 1, :]
        return acc

    gate = conv(gwin_ref, cwg_ref, bg_ref)
    up = conv(uwin_ref, cwu_ref, bu_ref)
    o_ref[...] = (_silu(gate) * up).astype(o_ref.dtype)
    gwin_ref[0:HALO, :] = gwin_ref[tm:tm + HALO, :]
    uwin_ref[0:HALO, :] = uwin_ref[tm:tm + HALO, :]


def ffn_up(hn, w_up, conv_w, conv_b, *, batch, seq, tm=1024, tn=512):
    ns = seq // tm
    wg = w_up[:, :FFN_DIM].astype(BF16)
    wu = w_up[:, FFN_DIM:].astype(BF16)
    padw = lambda w: jnp.zeros((HALO, FFN_DIM), F32).at[:FFN_CONV].set(w.astype(F32))
    row = lambda v: v.reshape(1, FFN_DIM).astype(F32)
    wspec = pl.BlockSpec((D_MODEL, tn), lambda j, b, s: (0, j))
    cspec = pl.BlockSpec((HALO, tn), lambda j, b, s: (0, j))
    bspec = pl.BlockSpec((1, tn), lambda j, b, s: (0, j))
    return pl.pallas_call(
        functools.partial(_ffn_up_kernel, tm=tm),
        grid=(FFN_DIM // tn, batch, ns),
        in_specs=[pl.BlockSpec((tm, D_MODEL), lambda j, b, s: (b * ns + s, 0)),
                  wspec, wspec, cspec, cspec, bspec, bspec],
        out_specs=pl.BlockSpec((tm, tn), lambda j, b, s: (b * ns + s, j)),
        out_shape=jax.ShapeDtypeStruct((batch * seq, FFN_DIM), BF16),
        scratch_shapes=[pltpu.VMEM((tm + HALO, tn), F32), pltpu.VMEM((tm + HALO, tn), F32)],
        compiler_params=_cparams(("arbitrary", "arbitrary", "arbitrary")),
        name="ffn_up",
    )(hn, wg, wu, padw(conv_w[:, :FFN_DIM]), padw(conv_w[:, FFN_DIM:]),
      row(conv_b[:FFN_DIM]), row(conv_b[FFN_DIM:]))


def _in_proj_weights(w_in):
    o = np.cumsum([0, 1024, 512, 512, 1024, 1024, 8, 8, 2048, 512, 512, 64, 8192])
    col = lambda i: w_in[:, int(o[i]):int(o[i + 1])]
    pool, q, k, v, z, a, b, conf, cq, ckv, krope, gates = (col(i) for i in range(12))
    w_a = jnp.concatenate([q, k, v, conf, pool, z, cq, ckv, gates], axis=1).astype(BF16)
    half = MLA_ROPE // 2
    zeros = lambda n: jnp.zeros((D_MODEL, n), w_in.dtype)
    w_c = jnp.concatenate([a, b, zeros(112), krope, zeros(64),
                           krope[:, half:], krope[:, :half], zeros(64)], axis=1).astype(BF16)
    return w_a, w_c


def kernel(x, positions, mix_norm, w_in, pool_w, pool_scale, gdn_conv_w, gdn_a_log, gdn_dt_bias, gdn_norm, conf_conv_w, conf_conv_b, conf_ln_g, conf_ln_b, mla_q_norm, mla_w_uq, mla_kv_norm, mla_w_ukv, w_pool_out, w_gdn_out, w_conf_out, w_mla_out, w_out, ffn_norm, ffn_w_up, ffn_conv_w, ffn_conv_b, ffn_w_down, final_norm):
    batch, seq, d = x.shape
    t = batch * seq
    xf = x.reshape(t, d)
    cos_t, sin_t = rope_tables(positions)
    for l in range(DEPTH):
        w_a, w_c = _in_proj_weights(w_in[l])
        proj_a, proj_c = in_proj(xf, mix_norm[l], w_a, w_c, tm=1024, tn=1024)
        h_a = pool_mixer(proj_a, pool_w[l], pool_scale[l], batch=batch, seq=seq)
        h_b = gated_deltanet(proj_a, proj_c, gdn_conv_w[l], gdn_a_log[l], gdn_dt_bias[l], gdn_norm[l],
                             batch=batch, seq=seq)
        h_c = conformer_conv(proj_a, conf_conv_w[l], conf_conv_b[l], conf_ln_g[l], conf_ln_b[l],
                             batch=batch, seq=seq)
        q, k, v = mla_prep(proj_a, proj_c, cos_t, sin_t, mla_q_norm[l], mla_kv_norm[l],
                           mla_w_uq[l], mla_w_ukv[l])
        h_d = flash_attention(q, k, v, batch=batch, seq=seq)
        w_branch = jnp.stack([w_pool_out[l], w_gdn_out[l], w_conf_out[l], w_mla_out[l]]).astype(BF16)
        merged = merge_branches(h_a, h_b, h_c, h_d, proj_a, w_branch)
        xf, hn = matmul(merged, w_out[l].astype(BF16), tm=512, tn=D_MODEL, tk=D_MODEL, out_dtype=F32,
                        residual=xf, norm_gain=ffn_norm[l], norm_dtype=BF16, name="out_proj")
        mid = ffn_up(hn, ffn_w_up[l], ffn_conv_w[l], ffn_conv_b[l], batch=batch, seq=seq)
        xf = matmul(mid, ffn_w_down[l].astype(BF16), tm=512, tn=1024, tk=FFN_DIM // 2, out_dtype=F32,
                    residual=xf, name="ffn_down")
    return rmsnorm(xf, final_norm, F32).reshape(batch, seq, d)
```

```python
import functools
import math

import numpy as np
import jax
import jax.numpy as jnp
from jax import lax
from jax.experimental import pallas as pl
from jax.experimental.pallas import tpu as pltpu

F32 = jnp.float32
BF16 = jnp.bfloat16

D_MODEL = 2048
DEPTH = 2
POOL_WINDOWS = (2, 4, 8, 16)
POOL_GROUPS = 4
POOL_GROUP_DIM = 256
POOL_WIDTH = 1024
GDN_K_HEADS = 4
GDN_V_HEADS = 8
GDN_HEAD_DIM = 128
GDN_KEY_WIDTH = 512
GDN_VAL_WIDTH = 1024
GDN_CONV_CH = 2048
GDN_CONV = 4
GDN_CHUNK = 64
CONF_WIDTH = 1024
CONF_CONV = 31
MLA_HEADS = 8
MLA_NOPE = 128
MLA_ROPE = 64
MLA_V = 128
MLA_QK = MLA_NOPE + MLA_ROPE
MLA_QKP = 256
MLA_Q_RANK = 512
MLA_KV_RANK = 512
ROPE_THETA = 10000.0
N_BRANCH = 4
FFN_DIM = 5632
FFN_CONV = 3
RMS_EPS = 1e-6
LN_EPS = 1e-5

VMEM_LIMIT_BYTES = 56 * 1024 * 1024
LANES = 128
HALO = 8
POOL_HALO = 16
CONF_HALO = 32

PA_QKV, PA_CONF, PA_POOL, PA_Z, PA_CQ, PA_CKV, PA_GATE = 0, 2048, 4096, 5120, 6144, 6656, 7168
PA_WIDTH = PA_GATE + N_BRANCH * D_MODEL
PC_WIDTH = 384


def _cparams(semantics):
    return pltpu.CompilerParams(dimension_semantics=semantics, vmem_limit_bytes=VMEM_LIMIT_BYTES)


def _sigmoid(x):
    return jax.nn.sigmoid(x)


def _silu(x):
    return x * jax.nn.sigmoid(x)


def _dot(a, b):
    return jnp.dot(a, b, preferred_element_type=F32)


def _rmsnorm_kernel(x_ref, g_ref, o_ref):
    x = x_ref[...]
    ms = jnp.mean(x * x, axis=-1, keepdims=True)
    o_ref[...] = (x * lax.rsqrt(ms + RMS_EPS) * g_ref[...]).astype(o_ref.dtype)


def rmsnorm(x, g, out_dtype, tm=512):
    m, d = x.shape
    return pl.pallas_call(
        _rmsnorm_kernel,
        grid=(m // tm,),
        in_specs=[pl.BlockSpec((tm, d), lambda i: (i, 0)), pl.BlockSpec((1, d), lambda i: (0, 0))],
        out_specs=pl.BlockSpec((tm, d), lambda i: (i, 0)),
        out_shape=jax.ShapeDtypeStruct((m, d), out_dtype),
        compiler_params=_cparams(("arbitrary",)),
        name="rmsnorm",
    )(x, g.reshape(1, d).astype(F32))


def _mm_kernel(*refs, nk, has_res, has_norm):
    refs = list(refs)
    a_ref, w_ref = refs[:2]
    pos = 2
    r_ref = g_ref = None
    if has_res:
        r_ref = refs[pos]
        pos += 1
    if has_norm:
        g_ref = refs[pos]
        pos += 1
    outs = refs[pos:-1]
    acc_ref = refs[-1]
    k = pl.program_id(2)
    part = _dot(a_ref[...], w_ref[...])

    def finish(acc):
        if r_ref is not None:
            acc = acc + r_ref[...]
        outs[0][...] = acc.astype(outs[0].dtype)
        if has_norm:
            y = acc * lax.rsqrt(jnp.mean(acc * acc, axis=-1, keepdims=True) + RMS_EPS) * g_ref[...]
            outs[1][...] = y.astype(outs[1].dtype)

    if nk == 1:
        finish(part)
    else:
        @pl.when(k == 0)
        def _():
            acc_ref[...] = part

        @pl.when(jnp.logical_and(k > 0, k < nk - 1))
        def _():
            acc_ref[...] += part

        @pl.when(k == nk - 1)
        def _():
            finish(acc_ref[...] + part)


def matmul(a, w, *, tm, tn, tk, out_dtype, residual=None, norm_gain=None, norm_dtype=None, cols_outer=False,
           name="matmul"):
    m, kdim = a.shape
    n = w.shape[1]
    nk = kdim // tk
    ij = (lambda g0, g1: (g1, g0)) if cols_outer else (lambda g0, g1: (g0, g1))
    in_specs = [pl.BlockSpec((tm, tk), lambda g0, g1, k: (ij(g0, g1)[0], k)),
                pl.BlockSpec((tk, tn), lambda g0, g1, k: (k, ij(g0, g1)[1]))]
    args = [a, w]
    if residual is not None:
        in_specs.append(pl.BlockSpec((tm, tn), lambda g0, g1, k: ij(g0, g1)))
        args.append(residual)
    out_spec = pl.BlockSpec((tm, tn), lambda g0, g1, k: ij(g0, g1))
    out_specs, out_shape = out_spec, jax.ShapeDtypeStruct((m, n), out_dtype)
    if norm_gain is not None:
        assert tn == n
        in_specs.append(pl.BlockSpec((1, n), lambda g0, g1, k: (0, 0)))
        args.append(norm_gain.reshape(1, n).astype(F32))
        out_specs = [out_spec, out_spec]
        out_shape = [out_shape, jax.ShapeDtypeStruct((m, n), norm_dtype)]
    return pl.pallas_call(
        functools.partial(_mm_kernel, nk=nk, has_res=residual is not None, has_norm=norm_gain is not None),
        grid=(n // tn, m // tm, nk) if cols_outer else (m // tm, n // tn, nk),
        in_specs=in_specs,
        out_specs=out_specs,
        out_shape=out_shape,
        scratch_shapes=[pltpu.VMEM((tm, tn) if nk > 1 else (HALO, LANES), F32)],
        compiler_params=_cparams(("arbitrary", "arbitrary", "arbitrary")),
        name=name,
    )(*args)


def _in_proj_kernel(x_ref, g_ref, w_ref, wc_ref, o_ref, oc_ref, xn_ref):
    j = pl.program_id(1)

    @pl.when(j == 0)
    def _():
        x = x_ref[...]
        xn = (x * lax.rsqrt(jnp.mean(x * x, axis=-1, keepdims=True) + RMS_EPS) * g_ref[...]).astype(BF16)
        xn_ref[...] = xn
        oc_ref[...] = _dot(xn, wc_ref[...])

    o_ref[...] = _dot(xn_ref[...], w_ref[...]).astype(o_ref.dtype)


def in_proj(x, gain, w, w_small, *, tm, tn):
    m, d = x.shape
    n = w.shape[1]
    ns = w_small.shape[1]
    return pl.pallas_call(
        _in_proj_kernel,
        grid=(m // tm, n // tn),
        in_specs=[pl.BlockSpec((tm, d), lambda i, j: (i, 0)),
                  pl.BlockSpec((1, d), lambda i, j: (0, 0)),
                  pl.BlockSpec((d, tn), lambda i, j: (0, j)),
                  pl.BlockSpec((d, ns), lambda i, j: (0, 0))],
        out_specs=[pl.BlockSpec((tm, tn), lambda i, j: (i, j)),
                   pl.BlockSpec((tm, ns), lambda i, j: (i, 0))],
        out_shape=[jax.ShapeDtypeStruct((m, n), BF16), jax.ShapeDtypeStruct((m, ns), F32)],
        scratch_shapes=[pltpu.VMEM((tm, d), BF16)],
        compiler_params=_cparams(("arbitrary", "arbitrary")),
        name="in_proj",
    )(x, gain.reshape(1, d).astype(F32), w, w_small)


def _pool_bands(ts):
    i = np.arange(ts)[:, None]
    m = np.arange(ts + POOL_HALO)[None, :]
    lag = (i + POOL_HALO) - m
    return np.stack([((lag >= 0) & (lag < w)) for w in POOL_WINDOWS]).astype(np.float32)


def _pool_kernel(u_ref, band_ref, pw_ref, scale_ref, o_ref, tail_ref, *, ts):
    s = pl.program_id(1)

    @pl.when(s == 0)
    def _():
        tail_ref[...] = jnp.zeros_like(tail_ref)

    t_abs = s * ts + lax.broadcasted_iota(jnp.int32, (ts, 1), 0)
    for g, win_len in enumerate(POOL_WINDOWS):
        cols = slice(g * POOL_GROUP_DIM, (g + 1) * POOL_GROUP_DIM)
        ug = u_ref[:, cols]
        win = jnp.concatenate([tail_ref[:, cols], ug], axis=0)
        sums = _dot(band_ref[g], win)
        cnt = jnp.minimum(t_abs + 1, win_len).astype(F32)
        diff = sums / cnt - ug.astype(F32)
        y = _dot(diff.astype(BF16), pw_ref[g]) * scale_ref[:, cols]
        o_ref[:, cols] = y.astype(o_ref.dtype)
    tail_ref[...] = u_ref[ts - POOL_HALO:, :]


def pool_mixer(proj_a, pool_w, pool_scale, *, batch, seq, ts=512):
    ns = seq // ts
    bands = jnp.asarray(_pool_bands(ts), BF16)
    cb = PA_POOL // POOL_WIDTH
    return pl.pallas_call(
        functools.partial(_pool_kernel, ts=ts),
        grid=(batch, ns),
        in_specs=[pl.BlockSpec((ts, POOL_WIDTH), lambda b, s: (b * ns + s, cb)),
                  pl.BlockSpec((POOL_GROUPS, ts, ts + POOL_HALO), lambda b, s: (0, 0, 0)),
                  pl.BlockSpec((POOL_GROUPS, POOL_GROUP_DIM, POOL_GROUP_DIM), lambda b, s: (0, 0, 0)),
                  pl.BlockSpec((1, POOL_WIDTH), lambda b, s: (0, 0))],
        out_specs=pl.BlockSpec((ts, POOL_WIDTH), lambda b, s: (b * ns + s, 0)),
        out_shape=jax.ShapeDtypeStruct((batch * seq, POOL_WIDTH), BF16),
        scratch_shapes=[pltpu.VMEM((POOL_HALO, POOL_WIDTH), BF16)],
        compiler_params=_cparams(("arbitrary", "arbitrary")),
        name="pool_mixer",
    )(proj_a, bands, pool_w.astype(BF16), pool_scale.reshape(1, POOL_WIDTH).astype(F32))


def _conf_kernel(u_ref, cw_ref, cb_ref, lg_ref, lb_ref, o_ref, win_ref, sh_ref, y_ref, *, ts, rc, rn):
    s = pl.program_id(1)
    span = ts + CONF_HALO - HALO

    @pl.when(s == 0)
    def _():
        win_ref[0:CONF_HALO, :] = jnp.zeros((CONF_HALO, CONF_WIDTH), F32)

    a = u_ref[:, :CONF_WIDTH].astype(F32)
    gate = u_ref[:, CONF_WIDTH:].astype(F32)
    win_ref[CONF_HALO:CONF_HALO + ts, :] = a * _sigmoid(gate)
    for r in range(1, HALO):
        sh_ref[r - 1, 0:span, :] = win_ref[r:r + span, :]
    off = CONF_HALO - (CONF_CONV - 1)

    for base in range(0, ts, rc):
        for c in range(CONF_WIDTH // LANES):
            lanes = slice(c * LANES, (c + 1) * LANES)
            acc = None
            for j in range(CONF_CONV):
                start = base + ((off + j) // HALO) * HALO
                r = (off + j) % HALO
                x = win_ref[start:start + rc, lanes] if r == 0 else sh_ref[r - 1, start:start + rc, lanes]
                term = x * cw_ref[j:j + 1, lanes]
                acc = term if acc is None else acc + term
            y_ref[base:base + rc, lanes] = acc + cb_ref[:, lanes]

    def norm_rows(i, carry):
        rows = pl.ds(pl.multiple_of(i * rn, rn), rn)
        h = y_ref[rows, :]
        mu = jnp.mean(h, axis=-1, keepdims=True)
        hc = h - mu
        var = jnp.mean(hc * hc, axis=-1, keepdims=True)
        y = hc * lax.rsqrt(var + LN_EPS) * lg_ref[...] + lb_ref[...]
        o_ref[rows, :] = _silu(y).astype(o_ref.dtype)
        return carry

    lax.fori_loop(0, ts // rn, norm_rows, 0, unroll=4)
    win_ref[0:CONF_HALO, :] = win_ref[ts:ts + CONF_HALO, :]


def conformer_conv(proj_a, conv_w, conv_b, ln_g, ln_b, *, batch, seq, ts=256, rc=128, rn=32):
    ns = seq // ts
    cw = jnp.zeros((CONF_HALO, CONF_WIDTH), F32).at[:CONF_CONV].set(conv_w.astype(F32))
    row = lambda v: v.reshape(1, CONF_WIDTH).astype(F32)
    cb = PA_CONF // (2 * CONF_WIDTH)
    const = lambda b, s: (0, 0)
    return pl.pallas_call(
        functools.partial(_conf_kernel, ts=ts, rc=rc, rn=rn),
        grid=(batch, ns),
        in_specs=[pl.BlockSpec((ts, 2 * CONF_WIDTH), lambda b, s: (b * ns + s, cb)),
                  pl.BlockSpec((CONF_HALO, CONF_WIDTH), const),
                  pl.BlockSpec((1, CONF_WIDTH), const),
                  pl.BlockSpec((1, CONF_WIDTH), const),
                  pl.BlockSpec((1, CONF_WIDTH), const)],
        out_specs=pl.BlockSpec((ts, CONF_WIDTH), lambda b, s: (b * ns + s, 0)),
        out_shape=jax.ShapeDtypeStruct((batch * seq, CONF_WIDTH), BF16),
        scratch_shapes=[pltpu.VMEM((ts + CONF_HALO, CONF_WIDTH), F32),
                        pltpu.VMEM((HALO - 1, ts + CONF_HALO - HALO, CONF_WIDTH), F32),
                        pltpu.VMEM((ts, CONF_WIDTH), F32)],
        compiler_params=_cparams(("arbitrary", "arbitrary")),
        name="conformer_conv",
    )(proj_a, cw, row(conv_b), row(ln_g), row(ln_b))


def _split(a):
    hi = a.astype(BF16)
    return hi, (a - hi.astype(F32)).astype(BF16)


def _split_dots(lhs_list, rhs):
    r_hi, r_lo = _split(rhs)
    parts = [_split(l) for l in lhs_list]
    his = [p[0] for p in parts]
    los = [p[1] for p in parts]
    n = len(lhs_list)
    m = lhs_list[0].shape[0]
    top = _dot(jnp.concatenate(his + los, axis=0), r_hi)
    bot = _dot(jnp.concatenate(his, axis=0) if n > 1 else his[0], r_lo)
    return [top[i * m:(i + 1) * m] + (top[(n + i) * m:(n + i + 1) * m] + bot[i * m:(i + 1) * m])
            for i in range(n)]


def _gdn_kernel(qkv_ref, z_ref, ab_ref, cw_ref, alog_ref, dtb_ref, ng_ref, o_ref, win_ref, state_ref, *, nck):
    c = GDN_CHUNK
    dh = GDN_HEAD_DIM
    rows = nck * c
    step = pl.program_id(1)

    @pl.when(step == 0)
    def _():
        win_ref[0:HALO, :] = jnp.zeros((HALO, GDN_CONV_CH), F32)
        state_ref[...] = jnp.zeros_like(state_ref)

    win_ref[HALO:HALO + rows, :] = qkv_ref[...].astype(F32)
    off = HALO - (GDN_CONV - 1)
    y = jnp.zeros((rows, GDN_CONV_CH), F32)
    for j in range(GDN_CONV):
        y = y + win_ref[off + j:off + j + rows, :] * cw_ref[j:j + 1, :]
    y = _silu(y)
    win_ref[0:HALO, :] = win_ref[rows:rows + HALO, :]

    ab = ab_ref[...]
    g_all = -jnp.exp(alog_ref[...]) * jnp.logaddexp(ab + dtb_ref[...], 0.0)
    beta_all = _sigmoid(ab)
    ii = lax.broadcasted_iota(jnp.int32, (c, c), 0)
    jj = lax.broadcasted_iota(jnp.int32, (c, c), 1)
    lower = ii >= jj
    strict = ii > jj
    tri = lower.astype(F32)
    eye = (ii == jj).astype(F32)

    def l2n(t):
        return t * lax.rsqrt(jnp.sum(t * t, axis=-1, keepdims=True) + 1e-6)

    qn = [l2n(y[:, kh * dh:(kh + 1) * dh]) * (dh ** -0.5) for kh in range(GDN_K_HEADS)]
    kn = [l2n(y[:, GDN_KEY_WIDTH + kh * dh:GDN_KEY_WIDTH + (kh + 1) * dh]) for kh in range(GDN_K_HEADS)]

    rep = GDN_V_HEADS // GDN_K_HEADS
    cks = range(nck)
    heads = range(GDN_V_HEADS)
    units = [(ck, h) for ck in cks for h in heads]
    rs = lambda ck: slice(ck * c, (ck + 1) * c)
    gc_all = [jnp.dot(tri, g_all[rs(ck)], precision=lax.Precision.HIGHEST, preferred_element_type=F32)
              for ck in cks]
    gc_t = [g.T for g in gc_all]
    q_c = {(ck, kh): qn[kh][rs(ck)] for ck in cks for kh in range(GDN_K_HEADS)}
    k_c = {(ck, kh): kn[kh][rs(ck)] for ck in cks for kh in range(GDN_K_HEADS)}
    kt_c = {key: val.T.astype(BF16) for key, val in k_c.items()}
    gcol = {(ck, h): gc_all[ck][:, h:h + 1] for ck, h in units}
    beta = {(ck, h): beta_all[rs(ck), 8 + h:9 + h] for ck, h in units}
    decay = {u: jnp.where(lower, jnp.exp(jnp.where(lower, gcol[u] - gc_t[u[0]][u[1]:u[1] + 1, :], 0.0)), 0.0)
             for u in units}
    kb = {(ck, h): k_c[ck, h // rep] * beta[ck, h] for ck, h in units}
    vb = {(ck, h): y[rs(ck), 2 * GDN_KEY_WIDTH + h * dh:2 * GDN_KEY_WIDTH + (h + 1) * dh] * beta[ck, h]
          for ck, h in units}
    kq = {(ck, kh): _dot(jnp.concatenate([kb[ck, kh * rep + r].astype(BF16) for r in range(rep)]
                                         + [q_c[ck, kh].astype(BF16)], axis=0), kt_c[ck, kh])
          for ck in cks for kh in range(GDN_K_HEADS)}
    lmat = {(ck, h): jnp.where(strict, kq[ck, h // rep][(h % rep) * c:(h % rep + 1) * c] * decay[ck, h], 0.0)
            for ck, h in units}
    qk = {(ck, h): kq[ck, h // rep][rep * c:(rep + 1) * c] * decay[ck, h] for ck, h in units}
    tinv = {u: eye - lmat[u] for u in units}
    pw = {u: _split_dots([lmat[u]], lmat[u])[0] for u in units}
    for _ in range(4):
        both = {u: _split_dots([pw[u], tinv[u]], pw[u]) for u in units}
        pw = {u: both[u][0] for u in units}
        tinv = {u: tinv[u] + both[u][1] for u in units}
    tinv = {u: tinv[u] + _split_dots([tinv[u]], pw[u])[0] for u in units}
    eg = {u: jnp.exp(gcol[u]) for u in units}
    uw = {u: _split_dots([tinv[u]], jnp.concatenate([vb[u], kb[u] * eg[u]], axis=1))[0] for u in units}
    qe = {(ck, h): (q_c[ck, h // rep] * eg[ck, h]).astype(BF16) for ck, h in units}
    glast = {u: gcol[u][c - 1:c, :] for u in units}
    dlast = {u: jnp.exp(glast[u] - gcol[u]) for u in units}

    state = [state_ref[h] for h in heads]
    for ck in cks:
        ws = [_dot(jnp.concatenate([uw[ck, h][:, dh:].astype(BF16), qe[ck, h]], axis=0),
                   state[h].astype(BF16)) for h in heads]
        v_new = [uw[ck, h][:, :dh] - ws[h][:c] for h in heads]
        o = [ws[h][c:] + _dot(qk[ck, h].astype(BF16), v_new[h].astype(BF16)) for h in heads]
        state = [state[h] * jnp.exp(glast[ck, h])
                 + _dot(kt_c[ck, h // rep], (v_new[h] * dlast[ck, h]).astype(BF16)) for h in heads]
        for h in heads:
            zh = z_ref[rs(ck), h * dh:(h + 1) * dh].astype(F32)
            on = o[h] * lax.rsqrt(jnp.mean(o[h] * o[h], axis=-1, keepdims=True) + RMS_EPS) * ng_ref[...]
            o_ref[rs(ck), h * dh:(h + 1) * dh] = (on * _silu(zh)).astype(o_ref.dtype)
    for h in heads:
        state_ref[h] = state[h]


def gated_deltanet(proj_a, proj_c, conv_w, a_log, dt_bias, norm_g, *, batch, seq, nck=4):
    rows = nck * GDN_CHUNK
    nc = seq // rows
    cw = jnp.zeros((HALO, GDN_CONV_CH), F32).at[:GDN_CONV].set(conv_w.astype(F32))
    pad = lambda v: jnp.zeros((1, LANES), F32).at[0, :GDN_V_HEADS].set(v.astype(F32))
    const = lambda b, s: (0, 0)
    return pl.pallas_call(
        functools.partial(_gdn_kernel, nck=nck),
        grid=(batch, nc),
        in_specs=[pl.BlockSpec((rows, GDN_CONV_CH), lambda b, s: (b * nc + s, PA_QKV // GDN_CONV_CH)),
                  pl.BlockSpec((rows, GDN_VAL_WIDTH), lambda b, s: (b * nc + s, PA_Z // GDN_VAL_WIDTH)),
                  pl.BlockSpec((rows, LANES), lambda b, s: (b * nc + s, 0)),
                  pl.BlockSpec((HALO, GDN_CONV_CH), const),
                  pl.BlockSpec((1, LANES), const),
                  pl.BlockSpec((1, LANES), const),
                  pl.BlockSpec((1, GDN_HEAD_DIM), const)],
        out_specs=pl.BlockSpec((rows, GDN_VAL_WIDTH), lambda b, s: (b * nc + s, 0)),
        out_shape=jax.ShapeDtypeStruct((batch * seq, GDN_VAL_WIDTH), BF16),
        scratch_shapes=[pltpu.VMEM((rows + HALO, GDN_CONV_CH), F32),
                        pltpu.VMEM((GDN_V_HEADS, GDN_HEAD_DIM, GDN_HEAD_DIM), F32)],
        compiler_params=_cparams(("arbitrary", "arbitrary")),
        name="gated_deltanet",
    )(proj_a, proj_a, proj_c, cw, pad(a_log), pad(dt_bias), norm_g.reshape(1, GDN_HEAD_DIM).astype(F32))


def _rope_kernel(pos_ref, freq_ref, sign_ref, c_ref, s_ref):
    tm = pos_ref.shape[0]
    ang = pos_ref[...].astype(F32) * freq_ref[...]
    c_ref[:, :MLA_NOPE] = jnp.ones((tm, MLA_NOPE), F32)
    c_ref[:, MLA_NOPE:] = jnp.cos(ang)
    s_ref[:, :MLA_NOPE] = jnp.zeros((tm, MLA_NOPE), F32)
    s_ref[:, MLA_NOPE:] = jnp.sin(ang) * sign_ref[...]


def rope_tables(positions, tm=512):
    t = positions.size
    inv_freq = ROPE_THETA ** (-jnp.arange(0, MLA_ROPE, 2, dtype=F32) / MLA_ROPE)
    width = MLA_QKP - MLA_NOPE
    pad = jnp.zeros((width - MLA_ROPE,), F32)
    freq = jnp.concatenate([inv_freq, inv_freq, pad]).reshape(1, width)
    half = MLA_ROPE // 2
    sign = jnp.concatenate([-jnp.ones((half,), F32), jnp.ones((half,), F32), pad]).reshape(1, width)
    return pl.pallas_call(
        _rope_kernel,
        grid=(t // tm,),
        in_specs=[pl.BlockSpec((tm, 1), lambda i: (i, 0)),
                  pl.BlockSpec((1, width), lambda i: (0, 0)),
                  pl.BlockSpec((1, width), lambda i: (0, 0))],
        out_specs=[pl.BlockSpec((tm, MLA_QKP), lambda i: (i, 0))] * 2,
        out_shape=[jax.ShapeDtypeStruct((t, MLA_QKP), F32)] * 2,
        compiler_params=_cparams(("arbitrary",)),
        name="rope_tables",
    )(positions.reshape(t, 1).astype(jnp.int32), freq, sign)


def _mla_prep_kernel(cq_ref, ckv_ref, kr_ref, krs_ref, cos_ref, sin_ref, qn_ref, kvn_ref,
                     wq_ref, wqs_ref, wkv_ref, q_out, k_out, v_out):
    def rms(x, g):
        return x * lax.rsqrt(jnp.mean(x * x, axis=-1, keepdims=True) + RMS_EPS) * g

    cqn = rms(cq_ref[...].astype(F32), qn_ref[...]).astype(BF16)
    ckvn = rms(ckv_ref[...].astype(F32), kvn_ref[...]).astype(BF16)
    cos = cos_ref[...]
    sin = sin_ref[...]
    kpe_t = (kr_ref[...] * cos[:, MLA_NOPE:] + krs_ref[...] * sin[:, MLA_NOPE:]).T.astype(k_out.dtype)
    q_all = _dot(cqn, wq_ref[...])
    qs_all = _dot(cqn, wqs_ref[...])
    kv_all = _dot(ckvn, wkv_ref[...])
    scale = (MLA_QK ** -0.5) * math.log2(math.e)
    for h in range(MLA_HEADS):
        cols = slice(h * MLA_QKP, (h + 1) * MLA_QKP)
        q_out[h] = ((q_all[:, cols] * cos + qs_all[:, cols] * sin) * scale).astype(q_out.dtype)
        base = h * (MLA_NOPE + MLA_V)
        k_out[h, :MLA_NOPE, :] = kv_all[:, base:base + MLA_NOPE].T.astype(k_out.dtype)
        k_out[h, MLA_NOPE:, :] = kpe_t
        v_out[h] = kv_all[:, base + MLA_NOPE:base + MLA_NOPE + MLA_V].astype(v_out.dtype)


def mla_prep(proj_a, proj_c, cos_t, sin_t, q_norm, kv_norm, w_uq, w_ukv, tm=512):
    t = proj_a.shape[0]
    hh = MLA_HEADS
    wq = w_uq.reshape(MLA_Q_RANK, hh, MLA_QK)
    half = MLA_ROPE // 2
    zeros = lambda n: jnp.zeros((MLA_Q_RANK, hh, n), w_uq.dtype)
    padw = lambda w: w.reshape(MLA_Q_RANK, hh * MLA_QKP).astype(BF16)
    wq_pad = padw(jnp.concatenate([wq, zeros(MLA_QKP - MLA_QK)], axis=-1))
    wq_swap = padw(jnp.concatenate([zeros(MLA_NOPE), wq[:, :, MLA_NOPE + half:],
                                    wq[:, :, MLA_NOPE:MLA_NOPE + half], zeros(MLA_QKP - MLA_QK)], axis=-1))
    row = lambda v: v.reshape(1, -1).astype(F32)
    const = lambda i: (0, 0)
    heads_tile = lambda i: (0, i, 0)
    return pl.pallas_call(
        _mla_prep_kernel,
        grid=(t // tm,),
        in_specs=[pl.BlockSpec((tm, MLA_Q_RANK), lambda i: (i, PA_CQ // MLA_Q_RANK)),
                  pl.BlockSpec((tm, MLA_KV_RANK), lambda i: (i, PA_CKV // MLA_KV_RANK)),
                  pl.BlockSpec((tm, LANES), lambda i: (i, 1)),
                  pl.BlockSpec((tm, LANES), lambda i: (i, 2)),
                  pl.BlockSpec((tm, MLA_QKP), lambda i: (i, 0)),
                  pl.BlockSpec((tm, MLA_QKP), lambda i: (i, 0)),
                  pl.BlockSpec((1, MLA_Q_RANK), const),
                  pl.BlockSpec((1, MLA_KV_RANK), const),
                  pl.BlockSpec((MLA_Q_RANK, hh * MLA_QKP), const),
                  pl.BlockSpec((MLA_Q_RANK, hh * MLA_QKP), const),
                  pl.BlockSpec((MLA_KV_RANK, hh * (MLA_NOPE + MLA_V)), const)],
        out_specs=[pl.BlockSpec((hh, tm, MLA_QKP), heads_tile),
                   pl.BlockSpec((hh, MLA_QKP, tm), lambda i: (0, 0, i)),
                   pl.BlockSpec((hh, tm, MLA_V), heads_tile)],
        out_shape=[jax.ShapeDtypeStruct((hh, t, MLA_QKP), BF16),
                   jax.ShapeDtypeStruct((hh, MLA_QKP, t), BF16),
                   jax.ShapeDtypeStruct((hh, t, MLA_V), BF16)],
        compiler_params=_cparams(("arbitrary",)),
        name="mla_prep",
    )(proj_a, proj_a, proj_c, proj_c, cos_t, sin_t, row(q_norm), row(kv_norm),
      wq_pad, wq_swap, w_ukv.astype(BF16))


def _flash_kernel(qi_tab, ki_tab, q_ref, k_ref, v_ref, o_ref, sa_ref, sb_ref, m_ref, acc_ref, *, tq, tk):
    p = pl.program_id(2)
    prev = jnp.maximum(p - 1, 0)
    qi = qi_tab[prev]
    ki = ki_tab[prev]
    last_k = (qi + 1) * (tq // tk) - 1
    even = p % 2 == 0
    needs_mask = (ki + 1) * tk > qi * tq + 1

    def update(masked, w_ref, r_ref):
        w_ref[...] = _dot(q_ref[0], k_ref[0])
        sc = r_ref[...]
        if masked:
            row = qi * tq + lax.broadcasted_iota(jnp.int32, (tq, tk), 0)
            col = ki * tk + lax.broadcasted_iota(jnp.int32, (tq, tk), 1)
            sc = jnp.where(col <= row, sc, -jnp.inf)
        v_ext = jnp.concatenate([v_ref[0], jnp.ones((tk, LANES), BF16)], axis=1)
        m_old = m_ref[...]
        m_new = jnp.maximum(m_old, jnp.max(sc, axis=-1, keepdims=True))
        alpha = jnp.exp2(m_old - m_new)
        pr = jnp.exp2(sc - m_new).astype(BF16)
        acc_ref[...] = alpha * acc_ref[...] + _dot(pr, v_ext)
        m_ref[...] = m_new

    @pl.when(p == 0)
    def _():
        sa_ref[...] = _dot(q_ref[0], k_ref[0])

    @pl.when(jnp.logical_and(p > 0, ki == 0))
    def _():
        m_ref[...] = jnp.full_like(m_ref, -jnp.inf)
        acc_ref[...] = jnp.zeros_like(acc_ref)

    for masked in (False, True):
        for is_even, w_ref, r_ref in ((True, sa_ref, sb_ref), (False, sb_ref, sa_ref)):
            cond = jnp.logical_and(p > 0, jnp.logical_and(needs_mask == masked, even == is_even))

            @pl.when(cond)
            def _(masked=masked, w_ref=w_ref, r_ref=r_ref):
                update(masked, w_ref, r_ref)

    @pl.when(jnp.logical_and(p > 0, ki == last_k))
    def _():
        acc = acc_ref[...]
        o_ref[...] = (acc[:, :MLA_V] / acc[:, MLA_V:MLA_V + 1]).astype(o_ref.dtype)


def flash_attention(q, k, v, *, batch, seq, tq=1024, tk=1024):
    nq = seq // tq
    nk = seq // tk
    pairs = [(a, b) for a in range(nq) for b in range((a + 1) * (tq // tk))]
    npairs = len(pairs)
    qi_tab = jnp.asarray([p[0] for p in pairs], jnp.int32)
    ki_tab = jnp.asarray([p[1] for p in pairs], jnp.int32)
    cur = lambda p: jnp.minimum(p, npairs - 1)
    prv = lambda p: jnp.maximum(p - 1, 0)
    scores = pltpu.VMEM((tq, tk), F32)
    grid_spec = pltpu.PrefetchScalarGridSpec(
        num_scalar_prefetch=2,
        grid=(batch, MLA_HEADS, npairs + 1),
        in_specs=[pl.BlockSpec((1, tq, MLA_QKP), lambda b, h, p, qt, kt: (h, b * nq + qt[cur(p)], 0)),
                  pl.BlockSpec((1, MLA_QKP, tk), lambda b, h, p, qt, kt: (h, 0, b * nk + kt[cur(p)])),
                  pl.BlockSpec((1, tk, MLA_V), lambda b, h, p, qt, kt: (h, b * nk + kt[prv(p)], 0))],
        out_specs=pl.BlockSpec((tq, MLA_V), lambda b, h, p, qt, kt: (b * nq + qt[prv(p)], h)),
        scratch_shapes=[scores, scores, pltpu.VMEM((tq, 1), F32), pltpu.VMEM((tq, MLA_V + LANES), F32)],
    )
    return pl.pallas_call(
        functools.partial(_flash_kernel, tq=tq, tk=tk),
        grid_spec=grid_spec,
        out_shape=jax.ShapeDtypeStruct((batch * seq, MLA_HEADS * MLA_V), BF16),
        compiler_params=_cparams(("arbitrary", "arbitrary", "arbitrary")),
        name="flash_attention",
    )(qi_tab, ki_tab, q, k, v)


def _merge_kernel(ha_ref, hb_ref, hc_ref, hd_ref, ga_ref, gb_ref, gc_ref, gd_ref, w_ref, o_ref):
    acc = None
    for b, (h_ref, g_ref) in enumerate(((ha_ref, ga_ref), (hb_ref, gb_ref), (hc_ref, gc_ref), (hd_ref, gd_ref))):
        gate = 0.5 * jnp.tanh(0.5 * g_ref[...].astype(F32)) + 0.5
        term = gate * _dot(h_ref[...], w_ref[b])
        acc = term if acc is None else acc + term
    o_ref[...] = acc.astype(o_ref.dtype)


def merge_branches(h_a, h_b, h_c, h_d, proj_a, w_branch, tm=1024, tn=512):
    t = h_a.shape[0]
    width = h_a.shape[1]
    g0 = PA_GATE // tn
    per = D_MODEL // tn
    hspec = pl.BlockSpec((tm, width), lambda i, j: (i, 0))
    gspecs = [pl.BlockSpec((tm, tn), functools.partial(lambda i, j, b: (i, g0 + b * per + j), b=b))
              for b in range(N_BRANCH)]
    return pl.pallas_call(
        _merge_kernel,
        grid=(t // tm, D_MODEL // tn),
        in_specs=[hspec] * 4 + gspecs + [pl.BlockSpec((N_BRANCH, width, tn), lambda i, j: (0, 0, j))],
        out_specs=pl.BlockSpec((tm, tn), lambda i, j: (i, j)),
        out_shape=jax.ShapeDtypeStruct((t, D_MODEL), BF16),
        compiler_params=_cparams(("arbitrary", "arbitrary")),
        name="merge_branches",
    )(h_a, h_b, h_c, h_d, proj_a, proj_a, proj_a, proj_a, w_branch)


def _ffn_up_kernel(x_ref, wg_ref, wu_ref, cwg_ref, cwu_ref, bg_ref, bu_ref, o_ref, gwin_ref, uwin_ref, *, tm):
    s = pl.program_id(2)

    @pl.when(s == 0)
    def _():
        gwin_ref[0:HALO, :] = jnp.zeros((HALO, gwin_ref.shape[1]), F32)
        uwin_ref[0:HALO, :] = jnp.zeros((HALO, uwin_ref.shape[1]), F32)

    x = x_ref[...]
    gwin_ref[HALO:HALO + tm, :] = _dot(x, wg_ref[...])
    uwin_ref[HALO:HALO + tm, :] = _dot(x, wu_ref[...])
    off = HALO - (FFN_CONV - 1)

    def conv(win_ref, cw_ref, b_ref):
        acc = b_ref[...] + win_ref[off:off + tm, :] * cw_ref[0:1, :]
        for j in range(1, FFN_CONV):
            acc = acc + win_ref[off + j:off + j + tm, :] * cw_ref[j:j + 1, :]
        return acc

    gate = conv(gwin_ref, cwg_ref, bg_ref)
    up = conv(uwin_ref, cwu_ref, bu_ref)
    o_ref[...] = (_silu(gate) * up).astype(o_ref.dtype)
    gwin_ref[0:HALO, :] = gwin_ref[tm:tm + HALO, :]
    uwin_ref[0:HALO, :] = uwin_ref[tm:tm + HALO, :]


def ffn_up(hn, w_up, conv_w, conv_b, *, batch, seq, tm=1024, tn=512):
    ns = seq // tm
    wg = w_up[:, :FFN_DIM].astype(BF16)
    wu = w_up[:, FFN_DIM:].astype(BF16)
    padw = lambda w: jnp.zeros((HALO, FFN_DIM), F32).at[:FFN_CONV].set(w.astype(F32))
    row = lambda v: v.reshape(1, FFN_DIM).astype(F32)
    wspec = pl.BlockSpec((D_MODEL, tn), lambda j, b, s: (0, j))
    cspec = pl.BlockSpec((HALO, tn), lambda j, b, s: (0, j))
    bspec = pl.BlockSpec((1, tn), lambda j, b, s: (0, j))
    return pl.pallas_call(
        functools.partial(_ffn_up_kernel, tm=tm),
        grid=(FFN_DIM // tn, batch, ns),
        in_specs=[pl.BlockSpec((tm, D_MODEL), lambda j, b, s: (b * ns + s, 0)),
                  wspec, wspec, cspec, cspec, bspec, bspec],
        out_specs=pl.BlockSpec((tm, tn), lambda j, b, s: (b * ns + s, j)),
        out_shape=jax.ShapeDtypeStruct((batch * seq, FFN_DIM), BF16),
        scratch_shapes=[pltpu.VMEM((tm + HALO, tn), F32), pltpu.VMEM((tm + HALO, tn), F32)],
        compiler_params=_cparams(("arbitrary", "arbitrary", "arbitrary")),
        name="ffn_up",
    )(hn, wg, wu, padw(conv_w[:, :FFN_DIM]), padw(conv_w[:, FFN_DIM:]),
      row(conv_b[:FFN_DIM]), row(conv_b[FFN_DIM:]))


def _in_proj_weights(w_in):
    o = np.cumsum([0, 1024, 512, 512, 1024, 1024, 8, 8, 2048, 512, 512, 64, 8192])
    col = lambda i: w_in[:, int(o[i]):int(o[i + 1])]
    pool, q, k, v, z, a, b, conf, cq, ckv, krope, gates = (col(i) for i in range(12))
    w_a = jnp.concatenate([q, k, v, conf, pool, z, cq, ckv, gates], axis=1).astype(BF16)
    half = MLA_ROPE // 2
    zeros = lambda n: jnp.zeros((D_MODEL, n), w_in.dtype)
    w_c = jnp.concatenate([a, b, zeros(112), krope, zeros(64),
                           krope[:, half:], krope[:, :half], zeros(64)], axis=1).astype(BF16)
    return w_a, w_c


def kernel(x, positions, mix_norm, w_in, pool_w, pool_scale, gdn_conv_w, gdn_a_log, gdn_dt_bias, gdn_norm, conf_conv_w, conf_conv_b, conf_ln_g, conf_ln_b, mla_q_norm, mla_w_uq, mla_kv_norm, mla_w_ukv, w_pool_out, w_gdn_out, w_conf_out, w_mla_out, w_out, ffn_norm, ffn_w_up, ffn_conv_w, ffn_conv_b, ffn_w_down, final_norm):
    batch, seq, d = x.shape
    t = batch * seq
    xf = x.reshape(t, d)
    cos_t, sin_t = rope_tables(positions)
    for l in range(DEPTH):
        w_a, w_c = _in_proj_weights(w_in[l])
        proj_a, proj_c = in_proj(xf, mix_norm[l], w_a, w_c, tm=1024, tn=1024)
        h_a = pool_mixer(proj_a, pool_w[l], pool_scale[l], batch=batch, seq=seq)
        h_b = gated_deltanet(proj_a, proj_c, gdn_conv_w[l], gdn_a_log[l], gdn_dt_bias[l], gdn_norm[l],
                             batch=batch, seq=seq)
        h_c = conformer_conv(proj_a, conf_conv_w[l], conf_conv_b[l], conf_ln_g[l], conf_ln_b[l],
                             batch=batch, seq=seq)
        q, k, v = mla_prep(proj_a, proj_c, cos_t, sin_t, mla_q_norm[l], mla_kv_norm[l],
                           mla_w_uq[l], mla_w_ukv[l])
        h_d = flash_attention(q, k, v, batch=batch, seq=seq)
        w_branch = jnp.stack([w_pool_out[l], w_gdn_out[l], w_conf_out[l], w_mla_out[l]]).astype(BF16)
        merged = merge_branches(h_a, h_b, h_c, h_d, proj_a, w_branch)
        xf, hn = matmul(merged, w_out[l].astype(BF16), tm=512, tn=D_MODEL, tk=D_MODEL, out_dtype=F32,
                        residual=xf, norm_gain=ffn_norm[l], norm_dtype=BF16, name="out_proj")
        mid = ffn_up(hn, ffn_w_up[l], ffn_conv_w[l], ffn_conv_b[l], batch=batch, seq=seq)
        xf = matmul(mid, ffn_w_down[l].astype(BF16), tm=512, tn=1024, tk=FFN_DIM, out_dtype=F32,
                    residual=xf, cols_outer=True, name="ffn_down")
    return rmsnorm(xf, final_norm, F32).reshape(batch, seq, d)
```

```python
import functools
import math

import numpy as np
import jax
import jax.numpy as jnp
from jax import lax
from jax.experimental import pallas as pl
from jax.experimental.pallas import tpu as pltpu

F32 = jnp.float32
BF16 = jnp.bfloat16

D_MODEL = 2048
DEPTH = 2
POOL_WINDOWS = (2, 4, 8, 16)
POOL_GROUPS = 4
POOL_GROUP_DIM = 256
POOL_WIDTH = 1024
GDN_K_HEADS = 4
GDN_V_HEADS = 8
GDN_HEAD_DIM = 128
GDN_KEY_WIDTH = 512
GDN_VAL_WIDTH = 1024
GDN_CONV_CH = 2048
GDN_CONV = 4
GDN_CHUNK = 64
CONF_WIDTH = 1024
CONF_CONV = 31
MLA_HEADS = 8
MLA_NOPE = 128
MLA_ROPE = 64
MLA_V = 128
MLA_QK = MLA_NOPE + MLA_ROPE
MLA_QKP = 256
MLA_Q_RANK = 512
MLA_KV_RANK = 512
ROPE_THETA = 10000.0
N_BRANCH = 4
FFN_DIM = 5632
FFN_CONV = 3
RMS_EPS = 1e-6
LN_EPS = 1e-5

VMEM_LIMIT_BYTES = 56 * 1024 * 1024
LANES = 128
HALO = 8
POOL_HALO = 16
CONF_HALO = 32

PA_QKV, PA_CONF, PA_POOL, PA_Z, PA_CQ, PA_CKV, PA_GATE = 0, 2048, 4096, 5120, 6144, 6656, 7168
PA_WIDTH = PA_GATE + N_BRANCH * D_MODEL
PC_WIDTH = 384

TILE_IN_PROJ = (1024, 1536)
TILE_MERGE = (1024, 512)
TILE_OUT_PROJ = (512, D_MODEL)
TILE_FFN_UP = (1024, 512)
TILE_FFN_DOWN = (512, 1024)


def _cparams(semantics):
    return pltpu.CompilerParams(dimension_semantics=semantics, vmem_limit_bytes=VMEM_LIMIT_BYTES)


def _sigmoid(x):
    return jax.nn.sigmoid(x)


def _silu(x):
    return x * jax.nn.sigmoid(x)


def _dot(a, b):
    return jnp.dot(a, b, preferred_element_type=F32)


def _rmsnorm_kernel(x_ref, g_ref, o_ref):
    x = x_ref[...]
    ms = jnp.mean(x * x, axis=-1, keepdims=True)
    o_ref[...] = (x * lax.rsqrt(ms + RMS_EPS) * g_ref[...]).astype(o_ref.dtype)


def rmsnorm(x, g, out_dtype, tm=512):
    m, d = x.shape
    return pl.pallas_call(
        _rmsnorm_kernel,
        grid=(m // tm,),
        in_specs=[pl.BlockSpec((tm, d), lambda i: (i, 0)), pl.BlockSpec((1, d), lambda i: (0, 0))],
        out_specs=pl.BlockSpec((tm, d), lambda i: (i, 0)),
        out_shape=jax.ShapeDtypeStruct((m, d), out_dtype),
        compiler_params=_cparams(("arbitrary",)),
        name="rmsnorm",
    )(x, g.reshape(1, d).astype(F32))


def _mm_kernel(*refs, nk, has_res, has_norm):
    refs = list(refs)
    a_ref, w_ref = refs[:2]
    pos = 2
    r_ref = g_ref = None
    if has_res:
        r_ref = refs[pos]
        pos += 1
    if has_norm:
        g_ref = refs[pos]
        pos += 1
    outs = refs[pos:-1]
    acc_ref = refs[-1]
    k = pl.program_id(2)
    part = _dot(a_ref[...], w_ref[...])

    def finish(acc):
        if r_ref is not None:
            acc = acc + r_ref[...]
        outs[0][...] = acc.astype(outs[0].dtype)
        if has_norm:
            y = acc * lax.rsqrt(jnp.mean(acc * acc, axis=-1, keepdims=True) + RMS_EPS) * g_ref[...]
            outs[1][...] = y.astype(outs[1].dtype)

    if nk == 1:
        finish(part)
    else:
        @pl.when(k == 0)
        def _():
            acc_ref[...] = part

        @pl.when(jnp.logical_and(k > 0, k < nk - 1))
        def _():
            acc_ref[...] += part

        @pl.when(k == nk - 1)
        def _():
            finish(acc_ref[...] + part)


def matmul(a, w, *, tm, tn, tk, out_dtype, residual=None, norm_gain=None, norm_dtype=None, cols_outer=False,
           name="matmul"):
    m, kdim = a.shape
    n = w.shape[1]
    nk = kdim // tk
    ij = (lambda g0, g1: (g1, g0)) if cols_outer else (lambda g0, g1: (g0, g1))
    in_specs = [pl.BlockSpec((tm, tk), lambda g0, g1, k: (ij(g0, g1)[0], k)),
                pl.BlockSpec((tk, tn), lambda g0, g1, k: (k, ij(g0, g1)[1]))]
    args = [a, w]
    if residual is not None:
        in_specs.append(pl.BlockSpec((tm, tn), lambda g0, g1, k: ij(g0, g1)))
        args.append(residual)
    out_spec = pl.BlockSpec((tm, tn), lambda g0, g1, k: ij(g0, g1))
    out_specs, out_shape = out_spec, jax.ShapeDtypeStruct((m, n), out_dtype)
    if norm_gain is not None:
        assert tn == n
        in_specs.append(pl.BlockSpec((1, n), lambda g0, g1, k: (0, 0)))
        args.append(norm_gain.reshape(1, n).astype(F32))
        out_specs = [out_spec, out_spec]
        out_shape = [out_shape, jax.ShapeDtypeStruct((m, n), norm_dtype)]
    return pl.pallas_call(
        functools.partial(_mm_kernel, nk=nk, has_res=residual is not None, has_norm=norm_gain is not None),
        grid=(n // tn, m // tm, nk) if cols_outer else (m // tm, n // tn, nk),
        in_specs=in_specs,
        out_specs=out_specs,
        out_shape=out_shape,
        scratch_shapes=[pltpu.VMEM((tm, tn) if nk > 1 else (HALO, LANES), F32)],
        compiler_params=_cparams(("arbitrary", "arbitrary", "arbitrary")),
        name=name,
    )(*args)


def _in_proj_kernel(x_ref, g_ref, w_ref, wc_ref, o_ref, oc_ref, xn_ref):
    j = pl.program_id(1)

    @pl.when(j == 0)
    def _():
        x = x_ref[...]
        xn = (x * lax.rsqrt(jnp.mean(x * x, axis=-1, keepdims=True) + RMS_EPS) * g_ref[...]).astype(BF16)
        xn_ref[...] = xn
        oc_ref[...] = _dot(xn, wc_ref[...])

    o_ref[...] = _dot(xn_ref[...], w_ref[...]).astype(o_ref.dtype)


def in_proj(x, gain, w, w_small, *, tm, tn):
    m, d = x.shape
    n = w.shape[1]
    ns = w_small.shape[1]
    return pl.pallas_call(
        _in_proj_kernel,
        grid=(m // tm, n // tn),
        in_specs=[pl.BlockSpec((tm, d), lambda i, j: (i, 0)),
                  pl.BlockSpec((1, d), lambda i, j: (0, 0)),
                  pl.BlockSpec((d, tn), lambda i, j: (0, j)),
                  pl.BlockSpec((d, ns), lambda i, j: (0, 0))],
        out_specs=[pl.BlockSpec((tm, tn), lambda i, j: (i, j)),
                   pl.BlockSpec((tm, ns), lambda i, j: (i, 0))],
        out_shape=[jax.ShapeDtypeStruct((m, n), BF16), jax.ShapeDtypeStruct((m, ns), F32)],
        scratch_shapes=[pltpu.VMEM((tm, d), BF16)],
        compiler_params=_cparams(("arbitrary", "arbitrary")),
        name="in_proj",
    )(x, gain.reshape(1, d).astype(F32), w, w_small)


def _pool_bands(ts):
    i = np.arange(ts)[:, None]
    m = np.arange(ts + POOL_HALO)[None, :]
    lag = (i + POOL_HALO) - m
    return np.stack([((lag >= 0) & (lag < w)) for w in POOL_WINDOWS]).astype(np.float32)


def _pool_kernel(u_ref, band_ref, pw_ref, scale_ref, o_ref, tail_ref, *, ts):
    s = pl.program_id(1)

    @pl.when(s == 0)
    def _():
        tail_ref[...] = jnp.zeros_like(tail_ref)

    t_abs = s * ts + lax.broadcasted_iota(jnp.int32, (ts, 1), 0)
    for g, win_len in enumerate(POOL_WINDOWS):
        cols = slice(g * POOL_GROUP_DIM, (g + 1) * POOL_GROUP_DIM)
        ug = u_ref[:, cols]
        win = jnp.concatenate([tail_ref[:, cols], ug], axis=0)
        sums = _dot(band_ref[g], win)
        cnt = jnp.minimum(t_abs + 1, win_len).astype(F32)
        diff = sums / cnt - ug.astype(F32)
        y = _dot(diff.astype(BF16), pw_ref[g]) * scale_ref[:, cols]
        o_ref[:, cols] = y.astype(o_ref.dtype)
    tail_ref[...] = u_ref[ts - POOL_HALO:, :]


def pool_mixer(proj_a, pool_w, pool_scale, *, batch, seq, ts=512):
    ns = seq // ts
    bands = jnp.asarray(_pool_bands(ts), BF16)
    cb = PA_POOL // POOL_WIDTH
    return pl.pallas_call(
        functools.partial(_pool_kernel, ts=ts),
        grid=(batch, ns),
        in_specs=[pl.BlockSpec((ts, POOL_WIDTH), lambda b, s: (b * ns + s, cb)),
                  pl.BlockSpec((POOL_GROUPS, ts, ts + POOL_HALO), lambda b, s: (0, 0, 0)),
                  pl.BlockSpec((POOL_GROUPS, POOL_GROUP_DIM, POOL_GROUP_DIM), lambda b, s: (0, 0, 0)),
                  pl.BlockSpec((1, POOL_WIDTH), lambda b, s: (0, 0))],
        out_specs=pl.BlockSpec((ts, POOL_WIDTH), lambda b, s: (b * ns + s, 0)),
        out_shape=jax.ShapeDtypeStruct((batch * seq, POOL_WIDTH), BF16),
        scratch_shapes=[pltpu.VMEM((POOL_HALO, POOL_WIDTH), BF16)],
        compiler_params=_cparams(("arbitrary", "arbitrary")),
        name="pool_mixer",
    )(proj_a, bands, pool_w.astype(BF16), pool_scale.reshape(1, POOL_WIDTH).astype(F32))


def _conf_kernel(u_ref, cw_ref, cb_ref, lg_ref, lb_ref, o_ref, win_ref, sh_ref, y_ref, *, ts, rc, rn):
    s = pl.program_id(1)
    span = ts + CONF_HALO - HALO

    @pl.when(s == 0)
    def _():
        win_ref[0:CONF_HALO, :] = jnp.zeros((CONF_HALO, CONF_WIDTH), F32)

    a = u_ref[:, :CONF_WIDTH].astype(F32)
    gate = u_ref[:, CONF_WIDTH:].astype(F32)
    win_ref[CONF_HALO:CONF_HALO + ts, :] = a * _sigmoid(gate)
    for r in range(1, HALO):
        sh_ref[r - 1, 0:span, :] = win_ref[r:r + span, :]
    off = CONF_HALO - (CONF_CONV - 1)

    for base in range(0, ts, rc):
        for c in range(CONF_WIDTH // LANES):
            lanes = slice(c * LANES, (c + 1) * LANES)
            acc = None
            for j in range(CONF_CONV):
                start = base + ((off + j) // HALO) * HALO
                r = (off + j) % HALO
                x = win_ref[start:start + rc, lanes] if r == 0 else sh_ref[r - 1, start:start + rc, lanes]
                term = x * cw_ref[j:j + 1, lanes]
                acc = term if acc is None else acc + term
            y_ref[base:base + rc, lanes] = acc + cb_ref[:, lanes]

    def norm_rows(i, carry):
        rows = pl.ds(pl.multiple_of(i * rn, rn), rn)
        h = y_ref[rows, :]
        mu = jnp.mean(h, axis=-1, keepdims=True)
        hc = h - mu
        var = jnp.mean(hc * hc, axis=-1, keepdims=True)
        y = hc * lax.rsqrt(var + LN_EPS) * lg_ref[...] + lb_ref[...]
        o_ref[rows, :] = _silu(y).astype(o_ref.dtype)
        return carry

    lax.fori_loop(0, ts // rn, norm_rows, 0, unroll=4)
    win_ref[0:CONF_HALO, :] = win_ref[ts:ts + CONF_HALO, :]


def conformer_conv(proj_a, conv_w, conv_b, ln_g, ln_b, *, batch, seq, ts=256, rc=128, rn=32):
    ns = seq // ts
    cw = jnp.zeros((CONF_HALO, CONF_WIDTH), F32).at[:CONF_CONV].set(conv_w.astype(F32))
    row = lambda v: v.reshape(1, CONF_WIDTH).astype(F32)
    cb = PA_CONF // (2 * CONF_WIDTH)
    const = lambda b, s: (0, 0)
    return pl.pallas_call(
        functools.partial(_conf_kernel, ts=ts, rc=rc, rn=rn),
        grid=(batch, ns),
        in_specs=[pl.BlockSpec((ts, 2 * CONF_WIDTH), lambda b, s: (b * ns + s, cb)),
                  pl.BlockSpec((CONF_HALO, CONF_WIDTH), const),
                  pl.BlockSpec((1, CONF_WIDTH), const),
                  pl.BlockSpec((1, CONF_WIDTH), const),
                  pl.BlockSpec((1, CONF_WIDTH), const)],
        out_specs=pl.BlockSpec((ts, CONF_WIDTH), lambda b, s: (b * ns + s, 0)),
        out_shape=jax.ShapeDtypeStruct((batch * seq, CONF_WIDTH), BF16),
        scratch_shapes=[pltpu.VMEM((ts + CONF_HALO, CONF_WIDTH), F32),
                        pltpu.VMEM((HALO - 1, ts + CONF_HALO - HALO, CONF_WIDTH), F32),
                        pltpu.VMEM((ts, CONF_WIDTH), F32)],
        compiler_params=_cparams(("arbitrary", "arbitrary")),
        name="conformer_conv",
    )(proj_a, cw, row(conv_b), row(ln_g), row(ln_b))


def _split(a):
    hi = a.astype(BF16)
    return hi, (a - hi.astype(F32)).astype(BF16)


def _split_dots(lhs_list, rhs):
    r_hi, r_lo = _split(rhs)
    parts = [_split(l) for l in lhs_list]
    his = [p[0] for p in parts]
    los = [p[1] for p in parts]
    n = len(lhs_list)
    m = lhs_list[0].shape[0]
    top = _dot(jnp.concatenate(his + los, axis=0), r_hi)
    bot = _dot(jnp.concatenate(his, axis=0) if n > 1 else his[0], r_lo)
    return [top[i * m:(i + 1) * m] + (top[(n + i) * m:(n + i + 1) * m] + bot[i * m:(i + 1) * m])
            for i in range(n)]


def _gdn_kernel(qkv_ref, z_ref, ab_ref, cw_ref, alog_ref, dtb_ref, ng_ref, o_ref, win_ref, state_ref, *, nck):
    c = GDN_CHUNK
    dh = GDN_HEAD_DIM
    rows = nck * c
    step = pl.program_id(1)

    @pl.when(step == 0)
    def _():
        win_ref[0:HALO, :] = jnp.zeros((HALO, GDN_CONV_CH), F32)
        state_ref[...] = jnp.zeros_like(state_ref)

    win_ref[HALO:HALO + rows, :] = qkv_ref[...].astype(F32)
    off = HALO - (GDN_CONV - 1)
    y = jnp.zeros((rows, GDN_CONV_CH), F32)
    for j in range(GDN_CONV):
        y = y + win_ref[off + j:off + j + rows, :] * cw_ref[j:j + 1, :]
    y = _silu(y)
    win_ref[0:HALO, :] = win_ref[rows:rows + HALO, :]

    ab = ab_ref[...]
    g_all = -jnp.exp(alog_ref[...]) * jnp.logaddexp(ab + dtb_ref[...], 0.0)
    beta_all = _sigmoid(ab)
    ii = lax.broadcasted_iota(jnp.int32, (c, c), 0)
    jj = lax.broadcasted_iota(jnp.int32, (c, c), 1)
    lower = ii >= jj
    strict = ii > jj
    tri = lower.astype(F32)
    eye = (ii == jj).astype(F32)

    def l2n(t):
        return t * lax.rsqrt(jnp.sum(t * t, axis=-1, keepdims=True) + 1e-6)

    qn = [l2n(y[:, kh * dh:(kh + 1) * dh]) * (dh ** -0.5) for kh in range(GDN_K_HEADS)]
    kn = [l2n(y[:, GDN_KEY_WIDTH + kh * dh:GDN_KEY_WIDTH + (kh + 1) * dh]) for kh in range(GDN_K_HEADS)]

    rep = GDN_V_HEADS // GDN_K_HEADS
    cks = range(nck)
    heads = range(GDN_V_HEADS)
    units = [(ck, h) for ck in cks for h in heads]
    rs = lambda ck: slice(ck * c, (ck + 1) * c)
    gc_all = [jnp.dot(tri, g_all[rs(ck)], precision=lax.Precision.HIGHEST, preferred_element_type=F32)
              for ck in cks]
    gc_t = [g.T for g in gc_all]
    q_c = {(ck, kh): qn[kh][rs(ck)] for ck in cks for kh in range(GDN_K_HEADS)}
    k_c = {(ck, kh): kn[kh][rs(ck)] for ck in cks for kh in range(GDN_K_HEADS)}
    kt_c = {key: val.T.astype(BF16) for key, val in k_c.items()}
    gcol = {(ck, h): gc_all[ck][:, h:h + 1] for ck, h in units}
    beta = {(ck, h): beta_all[rs(ck), 8 + h:9 + h] for ck, h in units}
    decay = {u: jnp.where(lower, jnp.exp(jnp.where(lower, gcol[u] - gc_t[u[0]][u[1]:u[1] + 1, :], 0.0)), 0.0)
             for u in units}
    kb = {(ck, h): k_c[ck, h // rep] * beta[ck, h] for ck, h in units}
    vb = {(ck, h): y[rs(ck), 2 * GDN_KEY_WIDTH + h * dh:2 * GDN_KEY_WIDTH + (h + 1) * dh] * beta[ck, h]
          for ck, h in units}
    kq = {(ck, kh): _dot(jnp.concatenate([kb[ck, kh * rep + r].astype(BF16) for r in range(rep)]
                                         + [q_c[ck, kh].astype(BF16)], axis=0), kt_c[ck, kh])
          for ck in cks for kh in range(GDN_K_HEADS)}
    lmat = {(ck, h): jnp.where(strict, kq[ck, h // rep][(h % rep) * c:(h % rep + 1) * c] * decay[ck, h], 0.0)
            for ck, h in units}
    qk = {(ck, h): kq[ck, h // rep][rep * c:(rep + 1) * c] * decay[ck, h] for ck, h in units}
    tinv = {u: eye - lmat[u] for u in units}
    pw = {u: _split_dots([lmat[u]], lmat[u])[0] for u in units}
    for _ in range(4):
        both = {u: _split_dots([pw[u], tinv[u]], pw[u]) for u in units}
        pw = {u: both[u][0] for u in units}
        tinv = {u: tinv[u] + both[u][1] for u in units}
    tinv = {u: tinv[u] + _split_dots([tinv[u]], pw[u])[0] for u in units}
    eg = {u: jnp.exp(gcol[u]) for u in units}
    uw = {u: _split_dots([tinv[u]], jnp.concatenate([vb[u], kb[u] * eg[u]], axis=1))[0] for u in units}
    qe = {(ck, h): (q_c[ck, h // rep] * eg[ck, h]).astype(BF16) for ck, h in units}
    glast = {u: gcol[u][c - 1:c, :] for u in units}
    dlast = {u: jnp.exp(glast[u] - gcol[u]) for u in units}

    state = [state_ref[h] for h in heads]
    for ck in cks:
        ws = [_dot(jnp.concatenate([uw[ck, h][:, dh:].astype(BF16), qe[ck, h]], axis=0),
                   state[h].astype(BF16)) for h in heads]
        v_new = [uw[ck, h][:, :dh] - ws[h][:c] for h in heads]
        o = [ws[h][c:] + _dot(qk[ck, h].astype(BF16), v_new[h].astype(BF16)) for h in heads]
        state = [state[h] * jnp.exp(glast[ck, h])
                 + _dot(kt_c[ck, h // rep], (v_new[h] * dlast[ck, h]).astype(BF16)) for h in heads]
        for h in heads:
            zh = z_ref[rs(ck), h * dh:(h + 1) * dh].astype(F32)
            on = o[h] * lax.rsqrt(jnp.mean(o[h] * o[h], axis=-1, keepdims=True) + RMS_EPS) * ng_ref[...]
            o_ref[rs(ck), h * dh:(h + 1) * dh] = (on * _silu(zh)).astype(o_ref.dtype)
    for h in heads:
        state_ref[h] = state[h]


def gated_deltanet(proj_a, proj_c, conv_w, a_log, dt_bias, norm_g, *, batch, seq, nck=4):
    rows = nck * GDN_CHUNK
    nc = seq // rows
    cw = jnp.zeros((HALO, GDN_CONV_CH), F32).at[:GDN_CONV].set(conv_w.astype(F32))
    pad = lambda v: jnp.zeros((1, LANES), F32).at[0, :GDN_V_HEADS].set(v.astype(F32))
    const = lambda b, s: (0, 0)
    return pl.pallas_call(
        functools.partial(_gdn_kernel, nck=nck),
        grid=(batch, nc),
        in_specs=[pl.BlockSpec((rows, GDN_CONV_CH), lambda b, s: (b * nc + s, PA_QKV // GDN_CONV_CH)),
                  pl.BlockSpec((rows, GDN_VAL_WIDTH), lambda b, s: (b * nc + s, PA_Z // GDN_VAL_WIDTH)),
                  pl.BlockSpec((rows, LANES), lambda b, s: (b * nc + s, 0)),
                  pl.BlockSpec((HALO, GDN_CONV_CH), const),
                  pl.BlockSpec((1, LANES), const),
                  pl.BlockSpec((1, LANES), const),
                  pl.BlockSpec((1, GDN_HEAD_DIM), const)],
        out_specs=pl.BlockSpec((rows, GDN_VAL_WIDTH), lambda b, s: (b * nc + s, 0)),
        out_shape=jax.ShapeDtypeStruct((batch * seq, GDN_VAL_WIDTH), BF16),
        scratch_shapes=[pltpu.VMEM((rows + HALO, GDN_CONV_CH), F32),
                        pltpu.VMEM((GDN_V_HEADS, GDN_HEAD_DIM, GDN_HEAD_DIM), F32)],
        compiler_params=_cparams(("arbitrary", "arbitrary")),
        name="gated_deltanet",
    )(proj_a, proj_a, proj_c, cw, pad(a_log), pad(dt_bias), norm_g.reshape(1, GDN_HEAD_DIM).astype(F32))


def _rope_kernel(pos_ref, freq_ref, sign_ref, c_ref, s_ref):
    tm = pos_ref.shape[0]
    ang = pos_ref[...].astype(F32) * freq_ref[...]
    c_ref[:, :MLA_NOPE] = jnp.ones((tm, MLA_NOPE), F32)
    c_ref[:, MLA_NOPE:] = jnp.cos(ang)
    s_ref[:, :MLA_NOPE] = jnp.zeros((tm, MLA_NOPE), F32)
    s_ref[:, MLA_NOPE:] = jnp.sin(ang) * sign_ref[...]


def rope_tables(positions, tm=512):
    t = positions.size
    inv_freq = ROPE_THETA ** (-jnp.arange(0, MLA_ROPE, 2, dtype=F32) / MLA_ROPE)
    width = MLA_QKP - MLA_NOPE
    pad = jnp.zeros((width - MLA_ROPE,), F32)
    freq = jnp.concatenate([inv_freq, inv_freq, pad]).reshape(1, width)
    half = MLA_ROPE // 2
    sign = jnp.concatenate([-jnp.ones((half,), F32), jnp.ones((half,), F32), pad]).reshape(1, width)
    return pl.pallas_call(
        _rope_kernel,
        grid=(t // tm,),
        in_specs=[pl.BlockSpec((tm, 1), lambda i: (i, 0)),
                  pl.BlockSpec((1, width), lambda i: (0, 0)),
                  pl.BlockSpec((1, width), lambda i: (0, 0))],
        out_specs=[pl.BlockSpec((tm, MLA_QKP), lambda i: (i, 0))] * 2,
        out_shape=[jax.ShapeDtypeStruct((t, MLA_QKP), F32)] * 2,
        compiler_params=_cparams(("arbitrary",)),
        name="rope_tables",
    )(positions.reshape(t, 1).astype(jnp.int32), freq, sign)


def _mla_prep_kernel(cq_ref, ckv_ref, kr_ref, krs_ref, cos_ref, sin_ref, qn_ref, kvn_ref,
                     wq_ref, wqs_ref, wkv_ref, q_out, k_out, v_out):
    def rms(x, g):
        return x * lax.rsqrt(jnp.mean(x * x, axis=-1, keepdims=True) + RMS_EPS) * g

    cqn = rms(cq_ref[...].astype(F32), qn_ref[...]).astype(BF16)
    ckvn = rms(ckv_ref[...].astype(F32), kvn_ref[...]).astype(BF16)
    cos = cos_ref[...]
    sin = sin_ref[...]
    kpe_t = (kr_ref[...] * cos[:, MLA_NOPE:] + krs_ref[...] * sin[:, MLA_NOPE:]).T.astype(k_out.dtype)
    q_all = _dot(cqn, wq_ref[...])
    qs_all = _dot(cqn, wqs_ref[...])
    kv_all = _dot(ckvn, wkv_ref[...])
    scale = (MLA_QK ** -0.5) * math.log2(math.e)
    for h in range(MLA_HEADS):
        cols = slice(h * MLA_QKP, (h + 1) * MLA_QKP)
        q_out[h] = ((q_all[:, cols] * cos + qs_all[:, cols] * sin) * scale).astype(q_out.dtype)
        base = h * (MLA_NOPE + MLA_V)
        k_out[h, :MLA_NOPE, :] = kv_all[:, base:base + MLA_NOPE].T.astype(k_out.dtype)
        k_out[h, MLA_NOPE:, :] = kpe_t
        v_out[h] = kv_all[:, base + MLA_NOPE:base + MLA_NOPE + MLA_V].astype(v_out.dtype)


def mla_prep(proj_a, proj_c, cos_t, sin_t, q_norm, kv_norm, w_uq, w_ukv, tm=512):
    t = proj_a.shape[0]
    hh = MLA_HEADS
    wq = w_uq.reshape(MLA_Q_RANK, hh, MLA_QK)
    half = MLA_ROPE // 2
    zeros = lambda n: jnp.zeros((MLA_Q_RANK, hh, n), w_uq.dtype)
    padw = lambda w: w.reshape(MLA_Q_RANK, hh * MLA_QKP).astype(BF16)
    wq_pad = padw(jnp.concatenate([wq, zeros(MLA_QKP - MLA_QK)], axis=-1))
    wq_swap = padw(jnp.concatenate([zeros(MLA_NOPE), wq[:, :, MLA_NOPE + half:],
                                    wq[:, :, MLA_NOPE:MLA_NOPE + half], zeros(MLA_QKP - MLA_QK)], axis=-1))
    row = lambda v: v.reshape(1, -1).astype(F32)
    const = lambda i: (0, 0)
    heads_tile = lambda i: (0, i, 0)
    return pl.pallas_call(
        _mla_prep_kernel,
        grid=(t // tm,),
        in_specs=[pl.BlockSpec((tm, MLA_Q_RANK), lambda i: (i, PA_CQ // MLA_Q_RANK)),
                  pl.BlockSpec((tm, MLA_KV_RANK), lambda i: (i, PA_CKV // MLA_KV_RANK)),
                  pl.BlockSpec((tm, LANES), lambda i: (i, 1)),
                  pl.BlockSpec((tm, LANES), lambda i: (i, 2)),
                  pl.BlockSpec((tm, MLA_QKP), lambda i: (i, 0)),
                  pl.BlockSpec((tm, MLA_QKP), lambda i: (i, 0)),
                  pl.BlockSpec((1, MLA_Q_RANK), const),
                  pl.BlockSpec((1, MLA_KV_RANK), const),
                  pl.BlockSpec((MLA_Q_RANK, hh * MLA_QKP), const),
                  pl.BlockSpec((MLA_Q_RANK, hh * MLA_QKP), const),
                  pl.BlockSpec((MLA_KV_RANK, hh * (MLA_NOPE + MLA_V)), const)],
        out_specs=[pl.BlockSpec((hh, tm, MLA_QKP), heads_tile),
                   pl.BlockSpec((hh, MLA_QKP, tm), lambda i: (0, 0, i)),
                   pl.BlockSpec((hh, tm, MLA_V), heads_tile)],
        out_shape=[jax.ShapeDtypeStruct((hh, t, MLA_QKP), BF16),
                   jax.ShapeDtypeStruct((hh, MLA_QKP, t), BF16),
                   jax.ShapeDtypeStruct((hh, t, MLA_V), BF16)],
        compiler_params=_cparams(("arbitrary",)),
        name="mla_prep",
    )(proj_a, proj_a, proj_c, proj_c, cos_t, sin_t, row(q_norm), row(kv_norm),
      wq_pad, wq_swap, w_ukv.astype(BF16))


def _flash_kernel(qi_tab, ki_tab, q_ref, k_ref, v_ref, o_ref, sa_ref, sb_ref, m_ref, acc_ref, *, tq, tk):
    p = pl.program_id(2)
    prev = jnp.maximum(p - 1, 0)
    qi = qi_tab[prev]
    ki = ki_tab[prev]
    last_k = (qi + 1) * (tq // tk) - 1
    even = p % 2 == 0
    needs_mask = (ki + 1) * tk > qi * tq + 1

    def update(masked, w_ref, r_ref):
        w_ref[...] = _dot(q_ref[0], k_ref[0])
        sc = r_ref[...]
        if masked:
            row = qi * tq + lax.broadcasted_iota(jnp.int32, (tq, tk), 0)
            col = ki * tk + lax.broadcasted_iota(jnp.int32, (tq, tk), 1)
            sc = jnp.where(col <= row, sc, -jnp.inf)
        v_ext = jnp.concatenate([v_ref[0], jnp.ones((tk, LANES), BF16)], axis=1)
        m_old = m_ref[...]
        m_new = jnp.maximum(m_old, jnp.max(sc, axis=-1, keepdims=True))
        alpha = jnp.exp2(m_old - m_new)
        pr = jnp.exp2(sc - m_new).astype(BF16)
        acc_ref[...] = alpha * acc_ref[...] + _dot(pr, v_ext)
        m_ref[...] = m_new

    @pl.when(p == 0)
    def _():
        sa_ref[...] = _dot(q_ref[0], k_ref[0])

    @pl.when(jnp.logical_and(p > 0, ki == 0))
    def _():
        m_ref[...] = jnp.full_like(m_ref, -jnp.inf)
        acc_ref[...] = jnp.zeros_like(acc_ref)

    for masked in (False, True):
        for is_even, w_ref, r_ref in ((True, sa_ref, sb_ref), (False, sb_ref, sa_ref)):
            cond = jnp.logical_and(p > 0, jnp.logical_and(needs_mask == masked, even == is_even))

            @pl.when(cond)
            def _(masked=masked, w_ref=w_ref, r_ref=r_ref):
                update(masked, w_ref, r_ref)

    @pl.when(jnp.logical_and(p > 0, ki == last_k))
    def _():
        acc = acc_ref[...]
        o_ref[...] = (acc[:, :MLA_V] / acc[:, MLA_V:MLA_V + 1]).astype(o_ref.dtype)


def flash_attention(q, k, v, *, batch, seq, tq=1024, tk=1024):
    nq = seq // tq
    nk = seq // tk
    pairs = [(a, b) for a in range(nq) for b in range((a + 1) * (tq // tk))]
    npairs = len(pairs)
    qi_tab = jnp.asarray([p[0] for p in pairs], jnp.int32)
    ki_tab = jnp.asarray([p[1] for p in pairs], jnp.int32)
    cur = lambda p: jnp.minimum(p, npairs - 1)
    prv = lambda p: jnp.maximum(p - 1, 0)
    scores = pltpu.VMEM((tq, tk), F32)
    grid_spec = pltpu.PrefetchScalarGridSpec(
        num_scalar_prefetch=2,
        grid=(batch, MLA_HEADS, npairs + 1),
        in_specs=[pl.BlockSpec((1, tq, MLA_QKP), lambda b, h, p, qt, kt: (h, b * nq + qt[cur(p)], 0)),
                  pl.BlockSpec((1, MLA_QKP, tk), lambda b, h, p, qt, kt: (h, 0, b * nk + kt[cur(p)])),
                  pl.BlockSpec((1, tk, MLA_V), lambda b, h, p, qt, kt: (h, b * nk + kt[prv(p)], 0))],
        out_specs=pl.BlockSpec((tq, MLA_V), lambda b, h, p, qt, kt: (b * nq + qt[prv(p)], h)),
        scratch_shapes=[scores, scores, pltpu.VMEM((tq, 1), F32), pltpu.VMEM((tq, MLA_V + LANES), F32)],
    )
    return pl.pallas_call(
        functools.partial(_flash_kernel, tq=tq, tk=tk),
        grid_spec=grid_spec,
        out_shape=jax.ShapeDtypeStruct((batch * seq, MLA_HEADS * MLA_V), BF16),
        compiler_params=_cparams(("arbitrary", "arbitrary", "arbitrary")),
        name="flash_attention",
    )(qi_tab, ki_tab, q, k, v)


def _merge_kernel(ha_ref, hb_ref, hc_ref, hd_ref, ga_ref, gb_ref, gc_ref, gd_ref, w_ref, o_ref):
    acc = None
    for b, (h_ref, g_ref) in enumerate(((ha_ref, ga_ref), (hb_ref, gb_ref), (hc_ref, gc_ref), (hd_ref, gd_ref))):
        gate = 0.5 * jnp.tanh(0.5 * g_ref[...].astype(F32)) + 0.5
        term = gate * _dot(h_ref[...], w_ref[b])
        acc = term if acc is None else acc + term
    o_ref[...] = acc.astype(o_ref.dtype)


def merge_branches(h_a, h_b, h_c, h_d, proj_a, w_branch, tm=1024, tn=512):
    t = h_a.shape[0]
    width = h_a.shape[1]
    g0 = PA_GATE // tn
    per = D_MODEL // tn
    hspec = pl.BlockSpec((tm, width), lambda i, j: (i, 0))
    gspecs = [pl.BlockSpec((tm, tn), functools.partial(lambda i, j, b: (i, g0 + b * per + j), b=b))
              for b in range(N_BRANCH)]
    return pl.pallas_call(
        _merge_kernel,
        grid=(t // tm, D_MODEL // tn),
        in_specs=[hspec] * 4 + gspecs + [pl.BlockSpec((N_BRANCH, width, tn), lambda i, j: (0, 0, j))],
        out_specs=pl.BlockSpec((tm, tn), lambda i, j: (i, j)),
        out_shape=jax.ShapeDtypeStruct((t, D_MODEL), BF16),
        compiler_params=_cparams(("arbitrary", "arbitrary")),
        name="merge_branches",
    )(h_a, h_b, h_c, h_d, proj_a, proj_a, proj_a, proj_a, w_branch)


def _ffn_up_kernel(x_ref, wg_ref, wu_ref, cwg_ref, cwu_ref, bg_ref, bu_ref, o_ref, gwin_ref, uwin_ref, *, tm):
    s = pl.program_id(2)

    @pl.when(s == 0)
    def _():
        gwin_ref[0:HALO, :] = jnp.zeros((HALO, gwin_ref.shape[1]), F32)
        uwin_ref[0:HALO, :] = jnp.zeros((HALO, uwin_ref.shape[1]), F32)

    x = x_ref[...]
    gwin_ref[HALO:HALO + tm, :] = _dot(x, wg_ref[...])
    uwin_ref[HALO:HALO + tm, :] = _dot(x, wu_ref[...])
    off = HALO - (FFN_CONV - 1)

    def conv(win_ref, cw_ref, b_ref):
        acc = b_ref[...] + win_ref[off:off + tm, :] * cw_ref[0:1, :]
        for j in range(1, FFN_CONV):
            acc = acc + win_ref[off + j:off + j + tm, :] * cw_ref[j:j + 1, :]
        return acc

    gate = conv(gwin_ref, cwg_ref, bg_ref)
    up = conv(uwin_ref, cwu_ref, bu_ref)
    o_ref[...] = (_silu(gate) * up).astype(o_ref.dtype)
    gwin_ref[0:HALO, :] = gwin_ref[tm:tm + HALO, :]
    uwin_ref[0:HALO, :] = uwin_ref[tm:tm + HALO, :]


def ffn_up(hn, w_up, conv_w, conv_b, *, batch, seq, tm=1024, tn=512):
    ns = seq // tm
    wg = w_up[:, :FFN_DIM].astype(BF16)
    wu = w_up[:, FFN_DIM:].astype(BF16)
    padw = lambda w: jnp.zeros((HALO, FFN_DIM), F32).at[:FFN_CONV].set(w.astype(F32))
    row = lambda v: v.reshape(1, FFN_DIM).astype(F32)
    wspec = pl.BlockSpec((D_MODEL, tn), lambda j, b, s: (0, j))
    cspec = pl.BlockSpec((HALO, tn), lambda j, b, s: (0, j))
    bspec = pl.BlockSpec((1, tn), lambda j, b, s: (0, j))
    return pl.pallas_call(
        functools.partial(_ffn_up_kernel, tm=tm),
        grid=(FFN_DIM // tn, batch, ns),
        in_specs=[pl.BlockSpec((tm, D_MODEL), lambda j, b, s: (b * ns + s, 0)),
                  wspec, wspec, cspec, cspec, bspec, bspec],
        out_specs=pl.BlockSpec((tm, tn), lambda j, b, s: (b * ns + s, j)),
        out_shape=jax.ShapeDtypeStruct((batch * seq, FFN_DIM), BF16),
        scratch_shapes=[pltpu.VMEM((tm + HALO, tn), F32), pltpu.VMEM((tm + HALO, tn), F32)],
        compiler_params=_cparams(("arbitrary", "arbitrary", "arbitrary")),
        name="ffn_up",
    )(hn, wg, wu, padw(conv_w[:, :FFN_DIM]), padw(conv_w[:, FFN_DIM:]),
      row(conv_b[:FFN_DIM]), row(conv_b[FFN_DIM:]))


def _in_proj_weights(w_in):
    o = np.cumsum([0, 1024, 512, 512, 1024, 1024, 8, 8, 2048, 512, 512, 64, 8192])
    col = lambda i: w_in[:, int(o[i]):int(o[i + 1])]
    pool, q, k, v, z, a, b, conf, cq, ckv, krope, gates = (col(i) for i in range(12))
    w_a = jnp.concatenate([q, k, v, conf, pool, z, cq, ckv, gates], axis=1).astype(BF16)
    half = MLA_ROPE // 2
    zeros = lambda n: jnp.zeros((D_MODEL, n), w_in.dtype)
    w_c = jnp.concatenate([a, b, zeros(112), krope, zeros(64),
                           krope[:, half:], krope[:, :half], zeros(64)], axis=1).astype(BF16)
    return w_a, w_c


def kernel(x, positions, mix_norm, w_in, pool_w, pool_scale, gdn_conv_w, gdn_a_log, gdn_dt_bias, gdn_norm, conf_conv_w, conf_conv_b, conf_ln_g, conf_ln_b, mla_q_norm, mla_w_uq, mla_kv_norm, mla_w_ukv, w_pool_out, w_gdn_out, w_conf_out, w_mla_out, w_out, ffn_norm, ffn_w_up, ffn_conv_w, ffn_conv_b, ffn_w_down, final_norm):
    batch, seq, d = x.shape
    t = batch * seq
    xf = x.reshape(t, d)
    cos_t, sin_t = rope_tables(positions)
    for l in range(DEPTH):
        w_a, w_c = _in_proj_weights(w_in[l])
        proj_a, proj_c = in_proj(xf, mix_norm[l], w_a, w_c, tm=TILE_IN_PROJ[0], tn=TILE_IN_PROJ[1])
        h_a = pool_mixer(proj_a, pool_w[l], pool_scale[l], batch=batch, seq=seq)
        h_b = gated_deltanet(proj_a, proj_c, gdn_conv_w[l], gdn_a_log[l], gdn_dt_bias[l], gdn_norm[l],
                             batch=batch, seq=seq)
        h_c = conformer_conv(proj_a, conf_conv_w[l], conf_conv_b[l], conf_ln_g[l], conf_ln_b[l],
                             batch=batch, seq=seq)
        q, k, v = mla_prep(proj_a, proj_c, cos_t, sin_t, mla_q_norm[l], mla_kv_norm[l],
                           mla_w_uq[l], mla_w_ukv[l])
        h_d = flash_attention(q, k, v, batch=batch, seq=seq)
        w_branch = jnp.stack([w_pool_out[l], w_gdn_out[l], w_conf_out[l], w_mla_out[l]]).astype(BF16)
        merged = merge_branches(h_a, h_b, h_c, h_d, proj_a, w_branch, tm=TILE_MERGE[0], tn=TILE_MERGE[1])
        xf, hn = matmul(merged, w_out[l].astype(BF16), tm=TILE_OUT_PROJ[0], tn=TILE_OUT_PROJ[1], tk=D_MODEL,
                        out_dtype=F32, residual=xf, norm_gain=ffn_norm[l], norm_dtype=BF16, name="out_proj")
        mid = ffn_up(hn, ffn_w_up[l], ffn_conv_w[l], ffn_conv_b[l], batch=batch, seq=seq,
                     tm=TILE_FFN_UP[0], tn=TILE_FFN_UP[1])
        xf = matmul(mid, ffn_w_down[l].astype(BF16), tm=TILE_FFN_DOWN[0], tn=TILE_FFN_DOWN[1], tk=FFN_DIM,
                    out_dtype=F32, residual=xf, cols_outer=True, name="ffn_down")
    return rmsnorm(xf, final_norm, F32).reshape(batch, seq, d)
```

```python
import functools
import math

import numpy as np
import jax
import jax.numpy as jnp
from jax import lax
from jax.experimental import pallas as pl
from jax.experimental.pallas import tpu as pltpu

F32 = jnp.float32
BF16 = jnp.bfloat16

D_MODEL = 2048
DEPTH = 2
POOL_WINDOWS = (2, 4, 8, 16)
POOL_GROUPS = 4
POOL_GROUP_DIM = 256
POOL_WIDTH = 1024
GDN_K_HEADS = 4
GDN_V_HEADS = 8
GDN_HEAD_DIM = 128
GDN_KEY_WIDTH = 512
GDN_VAL_WIDTH = 1024
GDN_CONV_CH = 2048
GDN_CONV = 4
GDN_CHUNK = 64
CONF_WIDTH = 1024
CONF_CONV = 31
MLA_HEADS = 8
MLA_NOPE = 128
MLA_ROPE = 64
MLA_V = 128
MLA_QK = MLA_NOPE + MLA_ROPE
MLA_QKP = 256
MLA_Q_RANK = 512
MLA_KV_RANK = 512
ROPE_THETA = 10000.0
N_BRANCH = 4
FFN_DIM = 5632
FFN_CONV = 3
RMS_EPS = 1e-6
LN_EPS = 1e-5

VMEM_LIMIT_BYTES = 56 * 1024 * 1024
LANES = 128
HALO = 8
POOL_HALO = 16
CONF_HALO = 32

PA_QKV, PA_CONF, PA_POOL, PA_Z, PA_CQ, PA_CKV, PA_GATE = 0, 2048, 4096, 5120, 6144, 6656, 7168
PA_WIDTH = PA_GATE + N_BRANCH * D_MODEL
PC_WIDTH = 384

TILE_IN_PROJ = (1024, 1536)
TILE_MERGE = (1024, 512)
TILE_OUT_PROJ = (512, D_MODEL)
TILE_FFN_UP = (1024, 512)
TILE_FFN_DOWN = (512, 1024)


def _cparams(semantics):
    return pltpu.CompilerParams(dimension_semantics=semantics, vmem_limit_bytes=VMEM_LIMIT_BYTES)


def _sigmoid(x):
    return jax.nn.sigmoid(x)


def _silu(x):
    return x * jax.nn.sigmoid(x)


def _dot(a, b):
    return jnp.dot(a, b, preferred_element_type=F32)


def _rmsnorm_kernel(x_ref, g_ref, o_ref):
    x = x_ref[...]
    ms = jnp.mean(x * x, axis=-1, keepdims=True)
    o_ref[...] = (x * lax.rsqrt(ms + RMS_EPS) * g_ref[...]).astype(o_ref.dtype)


def rmsnorm(x, g, out_dtype, tm=512):
    m, d = x.shape
    return pl.pallas_call(
        _rmsnorm_kernel,
        grid=(m // tm,),
        in_specs=[pl.BlockSpec((tm, d), lambda i: (i, 0)), pl.BlockSpec((1, d), lambda i: (0, 0))],
        out_specs=pl.BlockSpec((tm, d), lambda i: (i, 0)),
        out_shape=jax.ShapeDtypeStruct((m, d), out_dtype),
        compiler_params=_cparams(("arbitrary",)),
        name="rmsnorm",
    )(x, g.reshape(1, d).astype(F32))


def _mm_kernel(*refs, nk, has_res, has_norm):
    refs = list(refs)
    a_ref, w_ref = refs[:2]
    pos = 2
    r_ref = g_ref = None
    if has_res:
        r_ref = refs[pos]
        pos += 1
    if has_norm:
        g_ref = refs[pos]
        pos += 1
    outs = refs[pos:-1]
    acc_ref = refs[-1]
    k = pl.program_id(2)
    part = _dot(a_ref[...], w_ref[...])

    def finish(acc):
        if r_ref is not None:
            acc = acc + r_ref[...]
        outs[0][...] = acc.astype(outs[0].dtype)
        if has_norm:
            y = acc * lax.rsqrt(jnp.mean(acc * acc, axis=-1, keepdims=True) + RMS_EPS) * g_ref[...]
            outs[1][...] = y.astype(outs[1].dtype)

    if nk == 1:
        finish(part)
    else:
        @pl.when(k == 0)
        def _():
            acc_ref[...] = part

        @pl.when(jnp.logical_and(k > 0, k < nk - 1))
        def _():
            acc_ref[...] += part

        @pl.when(k == nk - 1)
        def _():
            finish(acc_ref[...] + part)


def matmul(a, w, *, tm, tn, tk, out_dtype, residual=None, norm_gain=None, norm_dtype=None, cols_outer=False,
           name="matmul"):
    m, kdim = a.shape
    n = w.shape[1]
    nk = kdim // tk
    ij = (lambda g0, g1: (g1, g0)) if cols_outer else (lambda g0, g1: (g0, g1))
    in_specs = [pl.BlockSpec((tm, tk), lambda g0, g1, k: (ij(g0, g1)[0], k)),
                pl.BlockSpec((tk, tn), lambda g0, g1, k: (k, ij(g0, g1)[1]))]
    args = [a, w]
    if residual is not None:
        in_specs.append(pl.BlockSpec((tm, tn), lambda g0, g1, k: ij(g0, g1)))
        args.append(residual)
    out_spec = pl.BlockSpec((tm, tn), lambda g0, g1, k: ij(g0, g1))
    out_specs, out_shape = out_spec, jax.ShapeDtypeStruct((m, n), out_dtype)
    if norm_gain is not None:
        assert tn == n
        in_specs.append(pl.BlockSpec((1, n), lambda g0, g1, k: (0, 0)))
        args.append(norm_gain.reshape(1, n).astype(F32))
        out_specs = [out_spec, out_spec]
        out_shape = [out_shape, jax.ShapeDtypeStruct((m, n), norm_dtype)]
    return pl.pallas_call(
        functools.partial(_mm_kernel, nk=nk, has_res=residual is not None, has_norm=norm_gain is not None),
        grid=(n // tn, m // tm, nk) if cols_outer else (m // tm, n // tn, nk),
        in_specs=in_specs,
        out_specs=out_specs,
        out_shape=out_shape,
        scratch_shapes=[pltpu.VMEM((tm, tn) if nk > 1 else (HALO, LANES), F32)],
        compiler_params=_cparams(("arbitrary", "arbitrary", "arbitrary")),
        name=name,
    )(*args)


def _in_proj_kernel(x_ref, g_ref, w_ref, wc_ref, o_ref, oc_ref, xn_ref):
    j = pl.program_id(1)

    @pl.when(j == 0)
    def _():
        x = x_ref[...]
        xn = (x * lax.rsqrt(jnp.mean(x * x, axis=-1, keepdims=True) + RMS_EPS) * g_ref[...]).astype(BF16)
        xn_ref[...] = xn
        oc_ref[...] = _dot(xn, wc_ref[...])

    o_ref[...] = _dot(xn_ref[...], w_ref[...]).astype(o_ref.dtype)


def in_proj(x, gain, w, w_small, *, tm, tn):
    m, d = x.shape
    n = w.shape[1]
    ns = w_small.shape[1]
    return pl.pallas_call(
        _in_proj_kernel,
        grid=(m // tm, n // tn),
        in_specs=[pl.BlockSpec((tm, d), lambda i, j: (i, 0)),
                  pl.BlockSpec((1, d), lambda i, j: (0, 0)),
                  pl.BlockSpec((d, tn), lambda i, j: (0, j)),
                  pl.BlockSpec((d, ns), lambda i, j: (0, 0))],
        out_specs=[pl.BlockSpec((tm, tn), lambda i, j: (i, j)),
                   pl.BlockSpec((tm, ns), lambda i, j: (i, 0))],
        out_shape=[jax.ShapeDtypeStruct((m, n), BF16), jax.ShapeDtypeStruct((m, ns), F32)],
        scratch_shapes=[pltpu.VMEM((tm, d), BF16)],
        compiler_params=_cparams(("arbitrary", "arbitrary")),
        name="in_proj",
    )(x, gain.reshape(1, d).astype(F32), w, w_small)


def _pool_bands(ts):
    i = np.arange(ts)[:, None]
    m = np.arange(ts + POOL_HALO)[None, :]
    lag = (i + POOL_HALO) - m
    return np.stack([((lag >= 0) & (lag < w)) for w in POOL_WINDOWS]).astype(np.float32)


def _pool_kernel(u_ref, band_ref, pw_ref, scale_ref, o_ref, tail_ref, *, ts):
    s = pl.program_id(1)

    @pl.when(s == 0)
    def _():
        tail_ref[...] = jnp.zeros_like(tail_ref)

    t_abs = s * ts + lax.broadcasted_iota(jnp.int32, (ts, 1), 0)
    for g, win_len in enumerate(POOL_WINDOWS):
        cols = slice(g * POOL_GROUP_DIM, (g + 1) * POOL_GROUP_DIM)
        ug = u_ref[:, cols]
        win = jnp.concatenate([tail_ref[:, cols], ug], axis=0)
        sums = _dot(band_ref[g], win)
        cnt = jnp.minimum(t_abs + 1, win_len).astype(F32)
        diff = sums / cnt - ug.astype(F32)
        y = _dot(diff.astype(BF16), pw_ref[g]) * scale_ref[:, cols]
        o_ref[:, cols] = y.astype(o_ref.dtype)
    tail_ref[...] = u_ref[ts - POOL_HALO:, :]


def pool_mixer(proj_a, pool_w, pool_scale, *, batch, seq, ts=512):
    ns = seq // ts
    bands = jnp.asarray(_pool_bands(ts), BF16)
    cb = PA_POOL // POOL_WIDTH
    return pl.pallas_call(
        functools.partial(_pool_kernel, ts=ts),
        grid=(batch, ns),
        in_specs=[pl.BlockSpec((ts, POOL_WIDTH), lambda b, s: (b * ns + s, cb)),
                  pl.BlockSpec((POOL_GROUPS, ts, ts + POOL_HALO), lambda b, s: (0, 0, 0)),
                  pl.BlockSpec((POOL_GROUPS, POOL_GROUP_DIM, POOL_GROUP_DIM), lambda b, s: (0, 0, 0)),
                  pl.BlockSpec((1, POOL_WIDTH), lambda b, s: (0, 0))],
        out_specs=pl.BlockSpec((ts, POOL_WIDTH), lambda b, s: (b * ns + s, 0)),
        out_shape=jax.ShapeDtypeStruct((batch * seq, POOL_WIDTH), BF16),
        scratch_shapes=[pltpu.VMEM((POOL_HALO, POOL_WIDTH), BF16)],
        compiler_params=_cparams(("arbitrary", "arbitrary")),
        name="pool_mixer",
    )(proj_a, bands, pool_w.astype(BF16), pool_scale.reshape(1, POOL_WIDTH).astype(F32))


def _conf_kernel(u_ref, cw_ref, cb_ref, lg_ref, lb_ref, o_ref, win_ref, sh_ref, y_ref, *, ts, rc, rn):
    s = pl.program_id(1)
    span = ts + CONF_HALO - HALO

    @pl.when(s == 0)
    def _():
        win_ref[0:CONF_HALO, :] = jnp.zeros((CONF_HALO, CONF_WIDTH), F32)

    a = u_ref[:, :CONF_WIDTH].astype(F32)
    gate = u_ref[:, CONF_WIDTH:].astype(F32)
    win_ref[CONF_HALO:CONF_HALO + ts, :] = a * _sigmoid(gate)
    for r in range(1, HALO):
        sh_ref[r - 1, 0:span, :] = win_ref[r:r + span, :]
    off = CONF_HALO - (CONF_CONV - 1)

    for base in range(0, ts, rc):
        for c in range(CONF_WIDTH // LANES):
            lanes = slice(c * LANES, (c + 1) * LANES)
            acc = None
            for j in range(CONF_CONV):
                start = base + ((off + j) // HALO) * HALO
                r = (off + j) % HALO
                x = win_ref[start:start + rc, lanes] if r == 0 else sh_ref[r - 1, start:start + rc, lanes]
                term = x * cw_ref[j:j + 1, lanes]
                acc = term if acc is None else acc + term
            y_ref[base:base + rc, lanes] = acc + cb_ref[:, lanes]

    def norm_rows(i, carry):
        rows = pl.ds(pl.multiple_of(i * rn, rn), rn)
        h = y_ref[rows, :]
        mu = jnp.mean(h, axis=-1, keepdims=True)
        hc = h - mu
        var = jnp.mean(hc * hc, axis=-1, keepdims=True)
        y = hc * lax.rsqrt(var + LN_EPS) * lg_ref[...] + lb_ref[...]
        o_ref[rows, :] = _silu(y).astype(o_ref.dtype)
        return carry

    lax.fori_loop(0, ts // rn, norm_rows, 0, unroll=4)
    win_ref[0:CONF_HALO, :] = win_ref[ts:ts + CONF_HALO, :]


def conformer_conv(proj_a, conv_w, conv_b, ln_g, ln_b, *, batch, seq, ts=256, rc=128, rn=32):
    ns = seq // ts
    cw = jnp.zeros((CONF_HALO, CONF_WIDTH), F32).at[:CONF_CONV].set(conv_w.astype(F32))
    row = lambda v: v.reshape(1, CONF_WIDTH).astype(F32)
    cb = PA_CONF // (2 * CONF_WIDTH)
    const = lambda b, s: (0, 0)
    return pl.pallas_call(
        functools.partial(_conf_kernel, ts=ts, rc=rc, rn=rn),
        grid=(batch, ns),
        in_specs=[pl.BlockSpec((ts, 2 * CONF_WIDTH), lambda b, s: (b * ns + s, cb)),
                  pl.BlockSpec((CONF_HALO, CONF_WIDTH), const),
                  pl.BlockSpec((1, CONF_WIDTH), const),
                  pl.BlockSpec((1, CONF_WIDTH), const),
                  pl.BlockSpec((1, CONF_WIDTH), const)],
        out_specs=pl.BlockSpec((ts, CONF_WIDTH), lambda b, s: (b * ns + s, 0)),
        out_shape=jax.ShapeDtypeStruct((batch * seq, CONF_WIDTH), BF16),
        scratch_shapes=[pltpu.VMEM((ts + CONF_HALO, CONF_WIDTH), F32),
                        pltpu.VMEM((HALO - 1, ts + CONF_HALO - HALO, CONF_WIDTH), F32),
                        pltpu.VMEM((ts, CONF_WIDTH), F32)],
        compiler_params=_cparams(("arbitrary", "arbitrary")),
        name="conformer_conv",
    )(proj_a, cw, row(conv_b), row(ln_g), row(ln_b))


def _split(a):
    hi = a.astype(BF16)
    return hi, (a - hi.astype(F32)).astype(BF16)


def _split_dots(lhs_list, rhs):
    r_hi, r_lo = _split(rhs)
    parts = [_split(l) for l in lhs_list]
    his = [p[0] for p in parts]
    los = [p[1] for p in parts]
    n = len(lhs_list)
    m = lhs_list[0].shape[0]
    top = _dot(jnp.concatenate(his + los, axis=0), r_hi)
    bot = _dot(jnp.concatenate(his, axis=0) if n > 1 else his[0], r_lo)
    return [top[i * m:(i + 1) * m] + (top[(n + i) * m:(n + i + 1) * m] + bot[i * m:(i + 1) * m])
            for i in range(n)]


def _gdn_kernel(qkv_ref, z_ref, ab_ref, cw_ref, alog_ref, dtb_ref, ng_ref, o_ref, win_ref, state_ref, *, nck):
    c = GDN_CHUNK
    dh = GDN_HEAD_DIM
    rows = nck * c
    step = pl.program_id(1)

    @pl.when(step == 0)
    def _():
        win_ref[0:HALO, :] = jnp.zeros((HALO, GDN_CONV_CH), F32)
        state_ref[...] = jnp.zeros_like(state_ref)

    win_ref[HALO:HALO + rows, :] = qkv_ref[...].astype(F32)
    off = HALO - (GDN_CONV - 1)
    y = jnp.zeros((rows, GDN_CONV_CH), F32)
    for j in range(GDN_CONV):
        y = y + win_ref[off + j:off + j + rows, :] * cw_ref[j:j + 1, :]
    y = _silu(y)
    win_ref[0:HALO, :] = win_ref[rows:rows + HALO, :]

    ab = ab_ref[...]
    g_all = -jnp.exp(alog_ref[...]) * jnp.logaddexp(ab + dtb_ref[...], 0.0)
    beta_all = _sigmoid(ab)
    ii = lax.broadcasted_iota(jnp.int32, (c, c), 0)
    jj = lax.broadcasted_iota(jnp.int32, (c, c), 1)
    lower = ii >= jj
    strict = ii > jj
    tri = lower.astype(F32)
    eye = (ii == jj).astype(F32)

    def l2n(t):
        return t * lax.rsqrt(jnp.sum(t * t, axis=-1, keepdims=True) + 1e-6)

    qn = [l2n(y[:, kh * dh:(kh + 1) * dh]) * (dh ** -0.5) for kh in range(GDN_K_HEADS)]
    kn = [l2n(y[:, GDN_KEY_WIDTH + kh * dh:GDN_KEY_WIDTH + (kh + 1) * dh]) for kh in range(GDN_K_HEADS)]

    rep = GDN_V_HEADS // GDN_K_HEADS
    cks = range(nck)
    heads = range(GDN_V_HEADS)
    units = [(ck, h) for ck in cks for h in heads]
    rs = lambda ck: slice(ck * c, (ck + 1) * c)
    gc_all = [jnp.dot(tri, g_all[rs(ck)], precision=lax.Precision.HIGHEST, preferred_element_type=F32)
              for ck in cks]
    gc_t = [g.T for g in gc_all]
    q_c = {(ck, kh): qn[kh][rs(ck)] for ck in cks for kh in range(GDN_K_HEADS)}
    k_c = {(ck, kh): kn[kh][rs(ck)] for ck in cks for kh in range(GDN_K_HEADS)}
    kt_c = {key: val.T.astype(BF16) for key, val in k_c.items()}
    gcol = {(ck, h): gc_all[ck][:, h:h + 1] for ck, h in units}
    beta = {(ck, h): beta_all[rs(ck), 8 + h:9 + h] for ck, h in units}
    decay = {u: jnp.where(lower, jnp.exp(jnp.where(lower, gcol[u] - gc_t[u[0]][u[1]:u[1] + 1, :], 0.0)), 0.0)
             for u in units}
    kb = {(ck, h): k_c[ck, h // rep] * beta[ck, h] for ck, h in units}
    vb = {(ck, h): y[rs(ck), 2 * GDN_KEY_WIDTH + h * dh:2 * GDN_KEY_WIDTH + (h + 1) * dh] * beta[ck, h]
          for ck, h in units}
    kq = {(ck, kh): _dot(jnp.concatenate([kb[ck, kh * rep + r].astype(BF16) for r in range(rep)]
                                         + [q_c[ck, kh].astype(BF16)], axis=0), kt_c[ck, kh])
          for ck in cks for kh in range(GDN_K_HEADS)}
    lmat = {(ck, h): jnp.where(strict, kq[ck, h // rep][(h % rep) * c:(h % rep + 1) * c] * decay[ck, h], 0.0)
            for ck, h in units}
    qk = {(ck, h): kq[ck, h // rep][rep * c:(rep + 1) * c] * decay[ck, h] for ck, h in units}
    tinv = {u: eye - lmat[u] for u in units}
    pw = {u: _split_dots([lmat[u]], lmat[u])[0] for u in units}
    for _ in range(4):
        both = {u: _split_dots([pw[u], tinv[u]], pw[u]) for u in units}
        pw = {u: both[u][0] for u in units}
        tinv = {u: tinv[u] + both[u][1] for u in units}
    tinv = {u: tinv[u] + _split_dots([tinv[u]], pw[u])[0] for u in units}
    eg = {u: jnp.exp(gcol[u]) for u in units}
    uw = {u: _split_dots([tinv[u]], jnp.concatenate([vb[u], kb[u] * eg[u]], axis=1))[0] for u in units}
    qe = {(ck, h): (q_c[ck, h // rep] * eg[ck, h]).astype(BF16) for ck, h in units}
    glast = {u: gcol[u][c - 1:c, :] for u in units}
    dlast = {u: jnp.exp(glast[u] - gcol[u]) for u in units}

    state = [state_ref[h] for h in heads]
    for ck in cks:
        ws = [_dot(jnp.concatenate([uw[ck, h][:, dh:].astype(BF16), qe[ck, h]], axis=0),
                   state[h].astype(BF16)) for h in heads]
        v_new = [uw[ck, h][:, :dh] - ws[h][:c] for h in heads]
        o = [ws[h][c:] + _dot(qk[ck, h].astype(BF16), v_new[h].astype(BF16)) for h in heads]
        state = [state[h] * jnp.exp(glast[ck, h])
                 + _dot(kt_c[ck, h // rep], (v_new[h] * dlast[ck, h]).astype(BF16)) for h in heads]
        for h in heads:
            zh = z_ref[rs(ck), h * dh:(h + 1) * dh].astype(F32)
            on = o[h] * lax.rsqrt(jnp.mean(o[h] * o[h], axis=-1, keepdims=True) + RMS_EPS) * ng_ref[...]
            o_ref[rs(ck), h * dh:(h + 1) * dh] = (on * _silu(zh)).astype(o_ref.dtype)
    for h in heads:
        state_ref[h] = state[h]


def gated_deltanet(proj_a, proj_c, conv_w, a_log, dt_bias, norm_g, *, batch, seq, nck=4):
    rows = nck * GDN_CHUNK
    nc = seq // rows
    cw = jnp.zeros((HALO, GDN_CONV_CH), F32).at[:GDN_CONV].set(conv_w.astype(F32))
    pad = lambda v: jnp.zeros((1, LANES), F32).at[0, :GDN_V_HEADS].set(v.astype(F32))
    const = lambda b, s: (0, 0)
    return pl.pallas_call(
        functools.partial(_gdn_kernel, nck=nck),
        grid=(batch, nc),
        in_specs=[pl.BlockSpec((rows, GDN_CONV_CH), lambda b, s: (b * nc + s, PA_QKV // GDN_CONV_CH)),
                  pl.BlockSpec((rows, GDN_VAL_WIDTH), lambda b, s: (b * nc + s, PA_Z // GDN_VAL_WIDTH)),
                  pl.BlockSpec((rows, LANES), lambda b, s: (b * nc + s, 0)),
                  pl.BlockSpec((HALO, GDN_CONV_CH), const),
                  pl.BlockSpec((1, LANES), const),
                  pl.BlockSpec((1, LANES), const),
                  pl.BlockSpec((1, GDN_HEAD_DIM), const)],
        out_specs=pl.BlockSpec((rows, GDN_VAL_WIDTH), lambda b, s: (b * nc + s, 0)),
        out_shape=jax.ShapeDtypeStruct((batch * seq, GDN_VAL_WIDTH), BF16),
        scratch_shapes=[pltpu.VMEM((rows + HALO, GDN_CONV_CH), F32),
                        pltpu.VMEM((GDN_V_HEADS, GDN_HEAD_DIM, GDN_HEAD_DIM), F32)],
        compiler_params=_cparams(("arbitrary", "arbitrary")),
        name="gated_deltanet",
    )(proj_a, proj_a, proj_c, cw, pad(a_log), pad(dt_bias), norm_g.reshape(1, GDN_HEAD_DIM).astype(F32))


def _rope_kernel(pos_ref, freq_ref, sign_ref, c_ref, s_ref):
    tm = pos_ref.shape[0]
    ang = pos_ref[...].astype(F32) * freq_ref[...]
    c_ref[:, :MLA_NOPE] = jnp.ones((tm, MLA_NOPE), F32)
    c_ref[:, MLA_NOPE:] = jnp.cos(ang)
    s_ref[:, :MLA_NOPE] = jnp.zeros((tm, MLA_NOPE), F32)
    s_ref[:, MLA_NOPE:] = jnp.sin(ang) * sign_ref[...]


def rope_tables(positions, tm=512):
    t = positions.size
    inv_freq = ROPE_THETA ** (-jnp.arange(0, MLA_ROPE, 2, dtype=F32) / MLA_ROPE)
    width = MLA_QKP - MLA_NOPE
    pad = jnp.zeros((width - MLA_ROPE,), F32)
    freq = jnp.concatenate([inv_freq, inv_freq, pad]).reshape(1, width)
    half = MLA_ROPE // 2
    sign = jnp.concatenate([-jnp.ones((half,), F32), jnp.ones((half,), F32), pad]).reshape(1, width)
    return pl.pallas_call(
        _rope_kernel,
        grid=(t // tm,),
        in_specs=[pl.BlockSpec((tm, 1), lambda i: (i, 0)),
                  pl.BlockSpec((1, width), lambda i: (0, 0)),
                  pl.BlockSpec((1, width), lambda i: (0, 0))],
        out_specs=[pl.BlockSpec((tm, MLA_QKP), lambda i: (i, 0))] * 2,
        out_shape=[jax.ShapeDtypeStruct((t, MLA_QKP), F32)] * 2,
        compiler_params=_cparams(("arbitrary",)),
        name="rope_tables",
    )(positions.reshape(t, 1).astype(jnp.int32), freq, sign)


def _mla_prep_kernel(cq_ref, ckv_ref, kr_ref, krs_ref, cos_ref, sin_ref, qn_ref, kvn_ref,
                     wq_ref, wqs_ref, wkv_ref, q_out, k_out, v_out):
    def rms(x, g):
        return x * lax.rsqrt(jnp.mean(x * x, axis=-1, keepdims=True) + RMS_EPS) * g

    cqn = rms(cq_ref[...].astype(F32), qn_ref[...]).astype(BF16)
    ckvn = rms(ckv_ref[...].astype(F32), kvn_ref[...]).astype(BF16)
    cos = cos_ref[...]
    sin = sin_ref[...]
    kpe_t = (kr_ref[...] * cos[:, MLA_NOPE:] + krs_ref[...] * sin[:, MLA_NOPE:]).T.astype(k_out.dtype)
    q_all = _dot(cqn, wq_ref[...])
    qs_all = _dot(cqn, wqs_ref[...])
    kv_all = _dot(ckvn, wkv_ref[...])
    scale = (MLA_QK ** -0.5) * math.log2(math.e)
    for h in range(MLA_HEADS):
        cols = slice(h * MLA_QKP, (h + 1) * MLA_QKP)
        q_out[h] = ((q_all[:, cols] * cos + qs_all[:, cols] * sin) * scale).astype(q_out.dtype)
        base = h * (MLA_NOPE + MLA_V)
        k_out[h, :MLA_NOPE, :] = kv_all[:, base:base + MLA_NOPE].T.astype(k_out.dtype)
        k_out[h, MLA_NOPE:, :] = kpe_t
        v_out[h] = kv_all[:, base + MLA_NOPE:base + MLA_NOPE + MLA_V].astype(v_out.dtype)


def mla_prep(proj_a, proj_c, cos_t, sin_t, q_norm, kv_norm, w_uq, w_ukv, tm=512):
    t = proj_a.shape[0]
    hh = MLA_HEADS
    wq = w_uq.reshape(MLA_Q_RANK, hh, MLA_QK)
    half = MLA_ROPE // 2
    zeros = lambda n: jnp.zeros((MLA_Q_RANK, hh, n), w_uq.dtype)
    padw = lambda w: w.reshape(MLA_Q_RANK, hh * MLA_QKP).astype(BF16)
    wq_pad = padw(jnp.concatenate([wq, zeros(MLA_QKP - MLA_QK)], axis=-1))
    wq_swap = padw(jnp.concatenate([zeros(MLA_NOPE), wq[:, :, MLA_NOPE + half:],
                                    wq[:, :, MLA_NOPE:MLA_NOPE + half], zeros(MLA_QKP - MLA_QK)], axis=-1))
    row = lambda v: v.reshape(1, -1).astype(F32)
    const = lambda i: (0, 0)
    heads_tile = lambda i: (0, i, 0)
    return pl.pallas_call(
        _mla_prep_kernel,
        grid=(t // tm,),
        in_specs=[pl.BlockSpec((tm, MLA_Q_RANK), lambda i: (i, PA_CQ // MLA_Q_RANK)),
                  pl.BlockSpec((tm, MLA_KV_RANK), lambda i: (i, PA_CKV // MLA_KV_RANK)),
                  pl.BlockSpec((tm, LANES), lambda i: (i, 1)),
                  pl.BlockSpec((tm, LANES), lambda i: (i, 2)),
                  pl.BlockSpec((tm, MLA_QKP), lambda i: (i, 0)),
                  pl.BlockSpec((tm, MLA_QKP), lambda i: (i, 0)),
                  pl.BlockSpec((1, MLA_Q_RANK), const),
                  pl.BlockSpec((1, MLA_KV_RANK), const),
                  pl.BlockSpec((MLA_Q_RANK, hh * MLA_QKP), const),
                  pl.BlockSpec((MLA_Q_RANK, hh * MLA_QKP), const),
                  pl.BlockSpec((MLA_KV_RANK, hh * (MLA_NOPE + MLA_V)), const)],
        out_specs=[pl.BlockSpec((hh, tm, MLA_QKP), heads_tile),
                   pl.BlockSpec((hh, MLA_QKP, tm), lambda i: (0, 0, i)),
                   pl.BlockSpec((hh, tm, MLA_V), heads_tile)],
        out_shape=[jax.ShapeDtypeStruct((hh, t, MLA_QKP), BF16),
                   jax.ShapeDtypeStruct((hh, MLA_QKP, t), BF16),
                   jax.ShapeDtypeStruct((hh, t, MLA_V), BF16)],
        compiler_params=_cparams(("arbitrary",)),
        name="mla_prep",
    )(proj_a, proj_a, proj_c, proj_c, cos_t, sin_t, row(q_norm), row(kv_norm),
      wq_pad, wq_swap, w_ukv.astype(BF16))


def _flash_kernel(qi_tab, ki_tab, q_ref, k_ref, v_ref, o_ref, sa_ref, sb_ref, m_ref, acc_ref, *, tq, tk, hb):
    p = pl.program_id(2)
    prev = jnp.maximum(p - 1, 0)
    qi = qi_tab[prev]
    ki = ki_tab[prev]
    last_k = (qi + 1) * (tq // tk) - 1
    even = p % 2 == 0
    needs_mask = (ki + 1) * tk > qi * tq + 1

    def update(masked, w_ref, r_ref):
        for h in range(hb):
            w_ref[h] = _dot(q_ref[h], k_ref[h])
        for h in range(hb):
            sc = r_ref[h]
            if masked:
                row = qi * tq + lax.broadcasted_iota(jnp.int32, (tq, tk), 0)
                col = ki * tk + lax.broadcasted_iota(jnp.int32, (tq, tk), 1)
                sc = jnp.where(col <= row, sc, -jnp.inf)
            v_ext = jnp.concatenate([v_ref[h], jnp.ones((tk, LANES), BF16)], axis=1)
            m_old = m_ref[h]
            m_new = jnp.maximum(m_old, jnp.max(sc, axis=-1, keepdims=True))
            alpha = jnp.exp2(m_old - m_new)
            pr = jnp.exp2(sc - m_new).astype(BF16)
            acc_ref[h] = alpha * acc_ref[h] + _dot(pr, v_ext)
            m_ref[h] = m_new

    @pl.when(p == 0)
    def _():
        for h in range(hb):
            sa_ref[h] = _dot(q_ref[h], k_ref[h])

    @pl.when(jnp.logical_and(p > 0, ki == 0))
    def _():
        m_ref[...] = jnp.full_like(m_ref, -jnp.inf)
        acc_ref[...] = jnp.zeros_like(acc_ref)

    for masked in (False, True):
        for is_even, w_ref, r_ref in ((True, sa_ref, sb_ref), (False, sb_ref, sa_ref)):
            cond = jnp.logical_and(p > 0, jnp.logical_and(needs_mask == masked, even == is_even))

            @pl.when(cond)
            def _(masked=masked, w_ref=w_ref, r_ref=r_ref):
                update(masked, w_ref, r_ref)

    @pl.when(jnp.logical_and(p > 0, ki == last_k))
    def _():
        for h in range(hb):
            acc = acc_ref[h]
            o_ref[:, h * MLA_V:(h + 1) * MLA_V] = (acc[:, :MLA_V] / acc[:, MLA_V:MLA_V + 1]).astype(o_ref.dtype)


def flash_attention(q, k, v, *, batch, seq, tq=1024, tk=1024, hb=2):
    nq = seq // tq
    nk = seq // tk
    pairs = [(a, b) for a in range(nq) for b in range((a + 1) * (tq // tk))]
    npairs = len(pairs)
    qi_tab = jnp.asarray([p[0] for p in pairs], jnp.int32)
    ki_tab = jnp.asarray([p[1] for p in pairs], jnp.int32)
    cur = lambda p: jnp.minimum(p, npairs - 1)
    prv = lambda p: jnp.maximum(p - 1, 0)
    scores = pltpu.VMEM((hb, tq, tk), F32)
    grid_spec = pltpu.PrefetchScalarGridSpec(
        num_scalar_prefetch=2,
        grid=(batch, MLA_HEADS // hb, npairs + 1),
        in_specs=[pl.BlockSpec((hb, tq, MLA_QKP), lambda b, h, p, qt, kt: (h, b * nq + qt[cur(p)], 0)),
                  pl.BlockSpec((hb, MLA_QKP, tk), lambda b, h, p, qt, kt: (h, 0, b * nk + kt[cur(p)])),
                  pl.BlockSpec((hb, tk, MLA_V), lambda b, h, p, qt, kt: (h, b * nk + kt[prv(p)], 0))],
        out_specs=pl.BlockSpec((tq, hb * MLA_V), lambda b, h, p, qt, kt: (b * nq + qt[prv(p)], h)),
        scratch_shapes=[scores, scores, pltpu.VMEM((hb, tq, 1), F32),
                        pltpu.VMEM((hb, tq, MLA_V + LANES), F32)],
    )
    return pl.pallas_call(
        functools.partial(_flash_kernel, tq=tq, tk=tk, hb=hb),
        grid_spec=grid_spec,
        out_shape=jax.ShapeDtypeStruct((batch * seq, MLA_HEADS * MLA_V), BF16),
        compiler_params=_cparams(("arbitrary", "arbitrary", "arbitrary")),
        name="flash_attention",
    )(qi_tab, ki_tab, q, k, v)


def _merge_kernel(ha_ref, hb_ref, hc_ref, hd_ref, ga_ref, gb_ref, gc_ref, gd_ref, w_ref, o_ref):
    acc = None
    for b, (h_ref, g_ref) in enumerate(((ha_ref, ga_ref), (hb_ref, gb_ref), (hc_ref, gc_ref), (hd_ref, gd_ref))):
        gate = 0.5 * jnp.tanh(0.5 * g_ref[...].astype(F32)) + 0.5
        term = gate * _dot(h_ref[...], w_ref[b])
        acc = term if acc is None else acc + term
    o_ref[...] = acc.astype(o_ref.dtype)


def merge_branches(h_a, h_b, h_c, h_d, proj_a, w_branch, tm=1024, tn=512):
    t = h_a.shape[0]
    width = h_a.shape[1]
    g0 = PA_GATE // tn
    per = D_MODEL // tn
    hspec = pl.BlockSpec((tm, width), lambda i, j: (i, 0))
    gspecs = [pl.BlockSpec((tm, tn), functools.partial(lambda i, j, b: (i, g0 + b * per + j), b=b))
              for b in range(N_BRANCH)]
    return pl.pallas_call(
        _merge_kernel,
        grid=(t // tm, D_MODEL // tn),
        in_specs=[hspec] * 4 + gspecs + [pl.BlockSpec((N_BRANCH, width, tn), lambda i, j: (0, 0, j))],
        out_specs=pl.BlockSpec((tm, tn), lambda i, j: (i, j)),
        out_shape=jax.ShapeDtypeStruct((t, D_MODEL), BF16),
        compiler_params=_cparams(("arbitrary", "arbitrary")),
        name="merge_branches",
    )(h_a, h_b, h_c, h_d, proj_a, proj_a, proj_a, proj_a, w_branch)


def _ffn_up_kernel(x_ref, wg_ref, wu_ref, cwg_ref, cwu_ref, bg_ref, bu_ref, o_ref, gwin_ref, uwin_ref, *, tm):
    s = pl.program_id(2)

    @pl.when(s == 0)
    def _():
        gwin_ref[0:HALO, :] = jnp.zeros((HALO, gwin_ref.shape[1]), F32)
        uwin_ref[0:HALO, :] = jnp.zeros((HALO, uwin_ref.shape[1]), F32)

    x = x_ref[...]
    gwin_ref[HALO:HALO + tm, :] = _dot(x, wg_ref[...])
    uwin_ref[HALO:HALO + tm, :] = _dot(x, wu_ref[...])
    off = HALO - (FFN_CONV - 1)

    def conv(win_ref, cw_ref, b_ref):
        acc = b_ref[...] + win_ref[off:off + tm, :] * cw_ref[0:1, :]
        for j in range(1, FFN_CONV):
            acc = acc + win_ref[off + j:off + j + tm, :] * cw_ref[j:j + 1, :]
        return acc

    gate = conv(gwin_ref, cwg_ref, bg_ref)
    up = conv(uwin_ref, cwu_ref, bu_ref)
    o_ref[...] = (_silu(gate) * up).astype(o_ref.dtype)
    gwin_ref[0:HALO, :] = gwin_ref[tm:tm + HALO, :]
    uwin_ref[0:HALO, :] = uwin_ref[tm:tm + HALO, :]


def ffn_up(hn, w_up, conv_w, conv_b, *, batch, seq, tm=1024, tn=512):
    ns = seq // tm
    wg = w_up[:, :FFN_DIM].astype(BF16)
    wu = w_up[:, FFN_DIM:].astype(BF16)
    padw = lambda w: jnp.zeros((HALO, FFN_DIM), F32).at[:FFN_CONV].set(w.astype(F32))
    row = lambda v: v.reshape(1, FFN_DIM).astype(F32)
    wspec = pl.BlockSpec((D_MODEL, tn), lambda j, b, s: (0, j))
    cspec = pl.BlockSpec((HALO, tn), lambda j, b, s: (0, j))
    bspec = pl.BlockSpec((1, tn), lambda j, b, s: (0, j))
    return pl.pallas_call(
        functools.partial(_ffn_up_kernel, tm=tm),
        grid=(FFN_DIM // tn, batch, ns),
        in_specs=[pl.BlockSpec((tm, D_MODEL), lambda j, b, s: (b * ns + s, 0)),
                  wspec, wspec, cspec, cspec, bspec, bspec],
        out_specs=pl.BlockSpec((tm, tn), lambda j, b, s: (b * ns + s, j)),
        out_shape=jax.ShapeDtypeStruct((batch * seq, FFN_DIM), BF16),
        scratch_shapes=[pltpu.VMEM((tm + HALO, tn), F32), pltpu.VMEM((tm + HALO, tn), F32)],
        compiler_params=_cparams(("arbitrary", "arbitrary", "arbitrary")),
        name="ffn_up",
    )(hn, wg, wu, padw(conv_w[:, :FFN_DIM]), padw(conv_w[:, FFN_DIM:]),
      row(conv_b[:FFN_DIM]), row(conv_b[FFN_DIM:]))


def _in_proj_weights(w_in):
    o = np.cumsum([0, 1024, 512, 512, 1024, 1024, 8, 8, 2048, 512, 512, 64, 8192])
    col = lambda i: w_in[:, int(o[i]):int(o[i + 1])]
    pool, q, k, v, z, a, b, conf, cq, ckv, krope, gates = (col(i) for i in range(12))
    w_a = jnp.concatenate([q, k, v, conf, pool, z, cq, ckv, gates], axis=1).astype(BF16)
    half = MLA_ROPE // 2
    zeros = lambda n: jnp.zeros((D_MODEL, n), w_in.dtype)
    w_c = jnp.concatenate([a, b, zeros(112), krope, zeros(64),
                           krope[:, half:], krope[:, :half], zeros(64)], axis=1).astype(BF16)
    return w_a, w_c


def kernel(x, positions, mix_norm, w_in, pool_w, pool_scale, gdn_conv_w, gdn_a_log, gdn_dt_bias, gdn_norm, conf_conv_w, conf_conv_b, conf_ln_g, conf_ln_b, mla_q_norm, mla_w_uq, mla_kv_norm, mla_w_ukv, w_pool_out, w_gdn_out, w_conf_out, w_mla_out, w_out, ffn_norm, ffn_w_up, ffn_conv_w, ffn_conv_b, ffn_w_down, final_norm):
    batch, seq, d = x.shape
    t = batch * seq
    xf = x.reshape(t, d)
    cos_t, sin_t = rope_tables(positions)
    for l in range(DEPTH):
        w_a, w_c = _in_proj_weights(w_in[l])
        proj_a, proj_c = in_proj(xf, mix_norm[l], w_a, w_c, tm=TILE_IN_PROJ[0], tn=TILE_IN_PROJ[1])
        h_a = pool_mixer(proj_a, pool_w[l], pool_scale[l], batch=batch, seq=seq)
        h_b = gated_deltanet(proj_a, proj_c, gdn_conv_w[l], gdn_a_log[l], gdn_dt_bias[l], gdn_norm[l],
                             batch=batch, seq=seq)
        h_c = conformer_conv(proj_a, conf_conv_w[l], conf_conv_b[l], conf_ln_g[l], conf_ln_b[l],
                             batch=batch, seq=seq)
        q, k, v = mla_prep(proj_a, proj_c, cos_t, sin_t, mla_q_norm[l], mla_kv_norm[l],
                           mla_w_uq[l], mla_w_ukv[l])
        h_d = flash_attention(q, k, v, batch=batch, seq=seq)
        w_branch = jnp.stack([w_pool_out[l], w_gdn_out[l], w_conf_out[l], w_mla_out[l]]).astype(BF16)
        merged = merge_branches(h_a, h_b, h_c, h_d, proj_a, w_branch, tm=TILE_MERGE[0], tn=TILE_MERGE[1])
        xf, hn = matmul(merged, w_out[l].astype(BF16), tm=TILE_OUT_PROJ[0], tn=TILE_OUT_PROJ[1], tk=D_MODEL,
                        out_dtype=F32, residual=xf, norm_gain=ffn_norm[l], norm_dtype=BF16, name="out_proj")
        mid = ffn_up(hn, ffn_w_up[l], ffn_conv_w[l], ffn_conv_b[l], batch=batch, seq=seq,
                     tm=TILE_FFN_UP[0], tn=TILE_FFN_UP[1])
        xf = matmul(mid, ffn_w_down[l].astype(BF16), tm=TILE_FFN_DOWN[0], tn=TILE_FFN_DOWN[1], tk=FFN_DIM,
                    out_dtype=F32, residual=xf, cols_outer=True, name="ffn_down")
    return rmsnorm(xf, final_norm, F32).reshape(batch, seq, d)
```

```python
import functools
import math

import numpy as np
import jax
import jax.numpy as jnp
from jax import lax
from jax.experimental import pallas as pl
from jax.experimental.pallas import tpu as pltpu

F32 = jnp.float32
BF16 = jnp.bfloat16

D_MODEL = 2048
DEPTH = 2
POOL_WINDOWS = (2, 4, 8, 16)
POOL_GROUPS = 4
POOL_GROUP_DIM = 256
POOL_WIDTH = 1024
GDN_K_HEADS = 4
GDN_V_HEADS = 8
GDN_HEAD_DIM = 128
GDN_KEY_WIDTH = 512
GDN_VAL_WIDTH = 1024
GDN_CONV_CH = 2048
GDN_CONV = 4
GDN_CHUNK = 64
CONF_WIDTH = 1024
CONF_CONV = 31
MLA_HEADS = 8
MLA_NOPE = 128
MLA_ROPE = 64
MLA_V = 128
MLA_QK = MLA_NOPE + MLA_ROPE
MLA_QKP = 256
MLA_Q_RANK = 512
MLA_KV_RANK = 512
ROPE_THETA = 10000.0
N_BRANCH = 4
FFN_DIM = 5632
FFN_CONV = 3
RMS_EPS = 1e-6
LN_EPS = 1e-5

VMEM_LIMIT_BYTES = 56 * 1024 * 1024
LANES = 128
HALO = 8
POOL_HALO = 16
CONF_HALO = 32

PA_QKV, PA_CONF, PA_POOL, PA_Z, PA_CQ, PA_CKV, PA_GATE = 0, 2048, 4096, 5120, 6144, 6656, 7168
PA_WIDTH = PA_GATE + N_BRANCH * D_MODEL
PC_WIDTH = 384

TILE_IN_PROJ = (1024, 1920)
TILE_MERGE = (1024, 512)
TILE_OUT_PROJ = (512, D_MODEL)
TILE_FFN_UP = (1024, 512)
TILE_FFN_DOWN = (512, 1024)


def _cparams(semantics):
    return pltpu.CompilerParams(dimension_semantics=semantics, vmem_limit_bytes=VMEM_LIMIT_BYTES)


def _sigmoid(x):
    return jax.nn.sigmoid(x)


def _silu(x):
    return x * jax.nn.sigmoid(x)


def _dot(a, b):
    return jnp.dot(a, b, preferred_element_type=F32)


def _rmsnorm_kernel(x_ref, g_ref, o_ref):
    x = x_ref[...]
    ms = jnp.mean(x * x, axis=-1, keepdims=True)
    o_ref[...] = (x * lax.rsqrt(ms + RMS_EPS) * g_ref[...]).astype(o_ref.dtype)


def rmsnorm(x, g, out_dtype, tm=512):
    m, d = x.shape
    return pl.pallas_call(
        _rmsnorm_kernel,
        grid=(m // tm,),
        in_specs=[pl.BlockSpec((tm, d), lambda i: (i, 0)), pl.BlockSpec((1, d), lambda i: (0, 0))],
        out_specs=pl.BlockSpec((tm, d), lambda i: (i, 0)),
        out_shape=jax.ShapeDtypeStruct((m, d), out_dtype),
        compiler_params=_cparams(("arbitrary",)),
        name="rmsnorm",
    )(x, g.reshape(1, d).astype(F32))


def _mm_kernel(*refs, nk, has_res, has_norm):
    refs = list(refs)
    a_ref, w_ref = refs[:2]
    pos = 2
    r_ref = g_ref = None
    if has_res:
        r_ref = refs[pos]
        pos += 1
    if has_norm:
        g_ref = refs[pos]
        pos += 1
    outs = refs[pos:-1]
    acc_ref = refs[-1]
    k = pl.program_id(2)
    part = _dot(a_ref[...], w_ref[...])

    def finish(acc):
        if r_ref is not None:
            acc = acc + r_ref[...]
        outs[0][...] = acc.astype(outs[0].dtype)
        if has_norm:
            y = acc * lax.rsqrt(jnp.mean(acc * acc, axis=-1, keepdims=True) + RMS_EPS) * g_ref[...]
            outs[1][...] = y.astype(outs[1].dtype)

    if nk == 1:
        finish(part)
    else:
        @pl.when(k == 0)
        def _():
            acc_ref[...] = part

        @pl.when(jnp.logical_and(k > 0, k < nk - 1))
        def _():
            acc_ref[...] += part

        @pl.when(k == nk - 1)
        def _():
            finish(acc_ref[...] + part)


def matmul(a, w, *, tm, tn, tk, out_dtype, residual=None, norm_gain=None, norm_dtype=None, cols_outer=False,
           name="matmul"):
    m, kdim = a.shape
    n = w.shape[1]
    nk = kdim // tk
    ij = (lambda g0, g1: (g1, g0)) if cols_outer else (lambda g0, g1: (g0, g1))
    in_specs = [pl.BlockSpec((tm, tk), lambda g0, g1, k: (ij(g0, g1)[0], k)),
                pl.BlockSpec((tk, tn), lambda g0, g1, k: (k, ij(g0, g1)[1]))]
    args = [a, w]
    if residual is not None:
        in_specs.append(pl.BlockSpec((tm, tn), lambda g0, g1, k: ij(g0, g1)))
        args.append(residual)
    out_spec = pl.BlockSpec((tm, tn), lambda g0, g1, k: ij(g0, g1))
    out_specs, out_shape = out_spec, jax.ShapeDtypeStruct((m, n), out_dtype)
    if norm_gain is not None:
        assert tn == n
        in_specs.append(pl.BlockSpec((1, n), lambda g0, g1, k: (0, 0)))
        args.append(norm_gain.reshape(1, n).astype(F32))
        out_specs = [out_spec, out_spec]
        out_shape = [out_shape, jax.ShapeDtypeStruct((m, n), norm_dtype)]
    return pl.pallas_call(
        functools.partial(_mm_kernel, nk=nk, has_res=residual is not None, has_norm=norm_gain is not None),
        grid=(n // tn, m // tm, nk) if cols_outer else (m // tm, n // tn, nk),
        in_specs=in_specs,
        out_specs=out_specs,
        out_shape=out_shape,
        scratch_shapes=[pltpu.VMEM((tm, tn) if nk > 1 else (HALO, LANES), F32)],
        compiler_params=_cparams(("arbitrary", "arbitrary", "arbitrary")),
        name=name,
    )(*args)


def _in_proj_kernel(x_ref, g_ref, w_ref, wc_ref, o_ref, oc_ref, xn_ref):
    j = pl.program_id(1)

    @pl.when(j == 0)
    def _():
        x = x_ref[...]
        xn = (x * lax.rsqrt(jnp.mean(x * x, axis=-1, keepdims=True) + RMS_EPS) * g_ref[...]).astype(BF16)
        xn_ref[...] = xn
        oc_ref[...] = _dot(xn, wc_ref[...])

    o_ref[...] = _dot(xn_ref[...], w_ref[...]).astype(o_ref.dtype)


def in_proj(x, gain, w, w_small, *, tm, tn):
    m, d = x.shape
    n = w.shape[1]
    ns = w_small.shape[1]
    return pl.pallas_call(
        _in_proj_kernel,
        grid=(m // tm, n // tn),
        in_specs=[pl.BlockSpec((tm, d), lambda i, j: (i, 0)),
                  pl.BlockSpec((1, d), lambda i, j: (0, 0)),
                  pl.BlockSpec((d, tn), lambda i, j: (0, j)),
                  pl.BlockSpec((d, ns), lambda i, j: (0, 0))],
        out_specs=[pl.BlockSpec((tm, tn), lambda i, j: (i, j)),
                   pl.BlockSpec((tm, ns), lambda i, j: (i, 0))],
        out_shape=[jax.ShapeDtypeStruct((m, n), BF16), jax.ShapeDtypeStruct((m, ns), F32)],
        scratch_shapes=[pltpu.VMEM((tm, d), BF16)],
        compiler_params=_cparams(("arbitrary", "arbitrary")),
        name="in_proj",
    )(x, gain.reshape(1, d).astype(F32), w, w_small)


def _pool_bands(ts):
    i = np.arange(ts)[:, None]
    m = np.arange(ts + POOL_HALO)[None, :]
    lag = (i + POOL_HALO) - m
    return np.stack([((lag >= 0) & (lag < w)) for w in POOL_WINDOWS]).astype(np.float32)


def _pool_kernel(u_ref, band_ref, pw_ref, scale_ref, o_ref, tail_ref, *, ts):
    s = pl.program_id(1)

    @pl.when(s == 0)
    def _():
        tail_ref[...] = jnp.zeros_like(tail_ref)

    t_abs = s * ts + lax.broadcasted_iota(jnp.int32, (ts, 1), 0)
    for g, win_len in enumerate(POOL_WINDOWS):
        cols = slice(g * POOL_GROUP_DIM, (g + 1) * POOL_GROUP_DIM)
        ug = u_ref[:, cols]
        win = jnp.concatenate([tail_ref[:, cols], ug], axis=0)
        sums = _dot(band_ref[g], win)
        cnt = jnp.minimum(t_abs + 1, win_len).astype(F32)
        diff = sums / cnt - ug.astype(F32)
        y = _dot(diff.astype(BF16), pw_ref[g]) * scale_ref[:, cols]
        o_ref[:, cols] = y.astype(o_ref.dtype)
    tail_ref[...] = u_ref[ts - POOL_HALO:, :]


def pool_mixer(proj_a, pool_w, pool_scale, *, batch, seq, ts=512):
    ns = seq // ts
    bands = jnp.asarray(_pool_bands(ts), BF16)
    cb = PA_POOL // POOL_WIDTH
    return pl.pallas_call(
        functools.partial(_pool_kernel, ts=ts),
        grid=(batch, ns),
        in_specs=[pl.BlockSpec((ts, POOL_WIDTH), lambda b, s: (b * ns + s, cb)),
                  pl.BlockSpec((POOL_GROUPS, ts, ts + POOL_HALO), lambda b, s: (0, 0, 0)),
                  pl.BlockSpec((POOL_GROUPS, POOL_GROUP_DIM, POOL_GROUP_DIM), lambda b, s: (0, 0, 0)),
                  pl.BlockSpec((1, POOL_WIDTH), lambda b, s: (0, 0))],
        out_specs=pl.BlockSpec((ts, POOL_WIDTH), lambda b, s: (b * ns + s, 0)),
        out_shape=jax.ShapeDtypeStruct((batch * seq, POOL_WIDTH), BF16),
        scratch_shapes=[pltpu.VMEM((POOL_HALO, POOL_WIDTH), BF16)],
        compiler_params=_cparams(("arbitrary", "arbitrary")),
        name="pool_mixer",
    )(proj_a, bands, pool_w.astype(BF16), pool_scale.reshape(1, POOL_WIDTH).astype(F32))


def _conf_kernel(u_ref, cw_ref, cb_ref, lg_ref, lb_ref, o_ref, win_ref, sh_ref, y_ref, *, ts, rc, rn):
    s = pl.program_id(1)
    span = ts + CONF_HALO - HALO

    @pl.when(s == 0)
    def _():
        win_ref[0:CONF_HALO, :] = jnp.zeros((CONF_HALO, CONF_WIDTH), F32)

    a = u_ref[:, :CONF_WIDTH].astype(F32)
    gate = u_ref[:, CONF_WIDTH:].astype(F32)
    win_ref[CONF_HALO:CONF_HALO + ts, :] = a * _sigmoid(gate)
    for r in range(1, HALO):
        sh_ref[r - 1, 0:span, :] = win_ref[r:r + span, :]
    off = CONF_HALO - (CONF_CONV - 1)

    for base in range(0, ts, rc):
        for c in range(CONF_WIDTH // LANES):
            lanes = slice(c * LANES, (c + 1) * LANES)
            acc = None
            for j in range(CONF_CONV):
                start = base + ((off + j) // HALO) * HALO
                r = (off + j) % HALO
                x = win_ref[start:start + rc, lanes] if r == 0 else sh_ref[r - 1, start:start + rc, lanes]
                term = x * cw_ref[j:j + 1, lanes]
                acc = term if acc is None else acc + term
            y_ref[base:base + rc, lanes] = acc + cb_ref[:, lanes]

    def norm_rows(i, carry):
        rows = pl.ds(pl.multiple_of(i * rn, rn), rn)
        h = y_ref[rows, :]
        mu = jnp.mean(h, axis=-1, keepdims=True)
        hc = h - mu
        var = jnp.mean(hc * hc, axis=-1, keepdims=True)
        y = hc * lax.rsqrt(var + LN_EPS) * lg_ref[...] + lb_ref[...]
        o_ref[rows, :] = _silu(y).astype(o_ref.dtype)
        return carry

    lax.fori_loop(0, ts // rn, norm_rows, 0, unroll=4)
    win_ref[0:CONF_HALO, :] = win_ref[ts:ts + CONF_HALO, :]


def conformer_conv(proj_a, conv_w, conv_b, ln_g, ln_b, *, batch, seq, ts=256, rc=128, rn=32):
    ns = seq // ts
    cw = jnp.zeros((CONF_HALO, CONF_WIDTH), F32).at[:CONF_CONV].set(conv_w.astype(F32))
    row = lambda v: v.reshape(1, CONF_WIDTH).astype(F32)
    cb = PA_CONF // (2 * CONF_WIDTH)
    const = lambda b, s: (0, 0)
    return pl.pallas_call(
        functools.partial(_conf_kernel, ts=ts, rc=rc, rn=rn),
        grid=(batch, ns),
        in_specs=[pl.BlockSpec((ts, 2 * CONF_WIDTH), lambda b, s: (b * ns + s, cb)),
                  pl.BlockSpec((CONF_HALO, CONF_WIDTH), const),
                  pl.BlockSpec((1, CONF_WIDTH), const),
                  pl.BlockSpec((1, CONF_WIDTH), const),
                  pl.BlockSpec((1, CONF_WIDTH), const)],
        out_specs=pl.BlockSpec((ts, CONF_WIDTH), lambda b, s: (b * ns + s, 0)),
        out_shape=jax.ShapeDtypeStruct((batch * seq, CONF_WIDTH), BF16),
        scratch_shapes=[pltpu.VMEM((ts + CONF_HALO, CONF_WIDTH), F32),
                        pltpu.VMEM((HALO - 1, ts + CONF_HALO - HALO, CONF_WIDTH), F32),
                        pltpu.VMEM((ts, CONF_WIDTH), F32)],
        compiler_params=_cparams(("arbitrary", "arbitrary")),
        name="conformer_conv",
    )(proj_a, cw, row(conv_b), row(ln_g), row(ln_b))


def _split(a):
    hi = a.astype(BF16)
    return hi, (a - hi.astype(F32)).astype(BF16)


def _split_dots(lhs_list, rhs):
    r_hi, r_lo = _split(rhs)
    parts = [_split(l) for l in lhs_list]
    his = [p[0] for p in parts]
    los = [p[1] for p in parts]
    n = len(lhs_list)
    m = lhs_list[0].shape[0]
    top = _dot(jnp.concatenate(his + los, axis=0), r_hi)
    bot = _dot(jnp.concatenate(his, axis=0) if n > 1 else his[0], r_lo)
    return [top[i * m:(i + 1) * m] + (top[(n + i) * m:(n + i + 1) * m] + bot[i * m:(i + 1) * m])
            for i in range(n)]


def _gdn_kernel(qkv_ref, z_ref, ab_ref, cw_ref, alog_ref, dtb_ref, ng_ref, o_ref, win_ref, state_ref, *, nck):
    c = GDN_CHUNK
    dh = GDN_HEAD_DIM
    rows = nck * c
    step = pl.program_id(1)

    @pl.when(step == 0)
    def _():
        win_ref[0:HALO, :] = jnp.zeros((HALO, GDN_CONV_CH), F32)
        state_ref[...] = jnp.zeros_like(state_ref)

    win_ref[HALO:HALO + rows, :] = qkv_ref[...].astype(F32)
    off = HALO - (GDN_CONV - 1)
    y = jnp.zeros((rows, GDN_CONV_CH), F32)
    for j in range(GDN_CONV):
        y = y + win_ref[off + j:off + j + rows, :] * cw_ref[j:j + 1, :]
    y = _silu(y)
    win_ref[0:HALO, :] = win_ref[rows:rows + HALO, :]

    ab = ab_ref[...]
    g_all = -jnp.exp(alog_ref[...]) * jnp.logaddexp(ab + dtb_ref[...], 0.0)
    beta_all = _sigmoid(ab)
    ii = lax.broadcasted_iota(jnp.int32, (c, c), 0)
    jj = lax.broadcasted_iota(jnp.int32, (c, c), 1)
    lower = ii >= jj
    strict = ii > jj
    tri = lower.astype(F32)
    eye = (ii == jj).astype(F32)

    def l2n(t):
        return t * lax.rsqrt(jnp.sum(t * t, axis=-1, keepdims=True) + 1e-6)

    qn = [l2n(y[:, kh * dh:(kh + 1) * dh]) * (dh ** -0.5) for kh in range(GDN_K_HEADS)]
    kn = [l2n(y[:, GDN_KEY_WIDTH + kh * dh:GDN_KEY_WIDTH + (kh + 1) * dh]) for kh in range(GDN_K_HEADS)]

    rep = GDN_V_HEADS // GDN_K_HEADS
    cks = range(nck)
    heads = range(GDN_V_HEADS)
    units = [(ck, h) for ck in cks for h in heads]
    rs = lambda ck: slice(ck * c, (ck + 1) * c)
    gc_all = [jnp.dot(tri, g_all[rs(ck)], precision=lax.Precision.HIGHEST, preferred_element_type=F32)
              for ck in cks]
    gc_t = [g.T for g in gc_all]
    q_c = {(ck, kh): qn[kh][rs(ck)] for ck in cks for kh in range(GDN_K_HEADS)}
    k_c = {(ck, kh): kn[kh][rs(ck)] for ck in cks for kh in range(GDN_K_HEADS)}
    kt_c = {key: val.T.astype(BF16) for key, val in k_c.items()}
    gcol = {(ck, h): gc_all[ck][:, h:h + 1] for ck, h in units}
    beta = {(ck, h): beta_all[rs(ck), 8 + h:9 + h] for ck, h in units}
    decay = {u: jnp.where(lower, jnp.exp(jnp.where(lower, gcol[u] - gc_t[u[0]][u[1]:u[1] + 1, :], 0.0)), 0.0)
             for u in units}
    kb = {(ck, h): k_c[ck, h // rep] * beta[ck, h] for ck, h in units}
    vb = {(ck, h): y[rs(ck), 2 * GDN_KEY_WIDTH + h * dh:2 * GDN_KEY_WIDTH + (h + 1) * dh] * beta[ck, h]
          for ck, h in units}
    kq = {(ck, kh): _dot(jnp.concatenate([kb[ck, kh * rep + r].astype(BF16) for r in range(rep)]
                                         + [q_c[ck, kh].astype(BF16)], axis=0), kt_c[ck, kh])
          for ck in cks for kh in range(GDN_K_HEADS)}
    lmat = {(ck, h): jnp.where(strict, kq[ck, h // rep][(h % rep) * c:(h % rep + 1) * c] * decay[ck, h], 0.0)
            for ck, h in units}
    qk = {(ck, h): kq[ck, h // rep][rep * c:(rep + 1) * c] * decay[ck, h] for ck, h in units}
    tinv = {u: eye - lmat[u] for u in units}
    pw = {u: _split_dots([lmat[u]], lmat[u])[0] for u in units}
    for _ in range(4):
        both = {u: _split_dots([pw[u], tinv[u]], pw[u]) for u in units}
        pw = {u: both[u][0] for u in units}
        tinv = {u: tinv[u] + both[u][1] for u in units}
    tinv = {u: tinv[u] + _split_dots([tinv[u]], pw[u])[0] for u in units}
    eg = {u: jnp.exp(gcol[u]) for u in units}
    uw = {u: _split_dots([tinv[u]], jnp.concatenate([vb[u], kb[u] * eg[u]], axis=1))[0] for u in units}
    qe = {(ck, h): (q_c[ck, h // rep] * eg[ck, h]).astype(BF16) for ck, h in units}
    glast = {u: gcol[u][c - 1:c, :] for u in units}
    dlast = {u: jnp.exp(glast[u] - gcol[u]) for u in units}

    state = [state_ref[h] for h in heads]
    for ck in cks:
        ws = [_dot(jnp.concatenate([uw[ck, h][:, dh:].astype(BF16), qe[ck, h]], axis=0),
                   state[h].astype(BF16)) for h in heads]
        v_new = [uw[ck, h][:, :dh] - ws[h][:c] for h in heads]
        o = [ws[h][c:] + _dot(qk[ck, h].astype(BF16), v_new[h].astype(BF16)) for h in heads]
        state = [state[h] * jnp.exp(glast[ck, h])
                 + _dot(kt_c[ck, h // rep], (v_new[h] * dlast[ck, h]).astype(BF16)) for h in heads]
        for h in heads:
            zh = z_ref[rs(ck), h * dh:(h + 1) * dh].astype(F32)
            on = o[h] * lax.rsqrt(jnp.mean(o[h] * o[h], axis=-1, keepdims=True) + RMS_EPS) * ng_ref[...]
            o_ref[rs(ck), h * dh:(h + 1) * dh] = (on * _silu(zh)).astype(o_ref.dtype)
    for h in heads:
        state_ref[h] = state[h]


def gated_deltanet(proj_a, proj_c, conv_w, a_log, dt_bias, norm_g, *, batch, seq, nck=4):
    rows = nck * GDN_CHUNK
    nc = seq // rows
    cw = jnp.zeros((HALO, GDN_CONV_CH), F32).at[:GDN_CONV].set(conv_w.astype(F32))
    pad = lambda v: jnp.zeros((1, LANES), F32).at[0, :GDN_V_HEADS].set(v.astype(F32))
    const = lambda b, s: (0, 0)
    return pl.pallas_call(
        functools.partial(_gdn_kernel, nck=nck),
        grid=(batch, nc),
        in_specs=[pl.BlockSpec((rows, GDN_CONV_CH), lambda b, s: (b * nc + s, PA_QKV // GDN_CONV_CH)),
                  pl.BlockSpec((rows, GDN_VAL_WIDTH), lambda b, s: (b * nc + s, PA_Z // GDN_VAL_WIDTH)),
                  pl.BlockSpec((rows, LANES), lambda b, s: (b * nc + s, 0)),
                  pl.BlockSpec((HALO, GDN_CONV_CH), const),
                  pl.BlockSpec((1, LANES), const),
                  pl.BlockSpec((1, LANES), const),
                  pl.BlockSpec((1, GDN_HEAD_DIM), const)],
        out_specs=pl.BlockSpec((rows, GDN_VAL_WIDTH), lambda b, s: (b * nc + s, 0)),
        out_shape=jax.ShapeDtypeStruct((batch * seq, GDN_VAL_WIDTH), BF16),
        scratch_shapes=[pltpu.VMEM((rows + HALO, GDN_CONV_CH), F32),
                        pltpu.VMEM((GDN_V_HEADS, GDN_HEAD_DIM, GDN_HEAD_DIM), F32)],
        compiler_params=_cparams(("arbitrary", "arbitrary")),
        name="gated_deltanet",
    )(proj_a, proj_a, proj_c, cw, pad(a_log), pad(dt_bias), norm_g.reshape(1, GDN_HEAD_DIM).astype(F32))


def _rope_kernel(pos_ref, freq_ref, sign_ref, c_ref, s_ref):
    tm = pos_ref.shape[0]
    ang = pos_ref[...].astype(F32) * freq_ref[...]
    c_ref[:, :MLA_NOPE] = jnp.ones((tm, MLA_NOPE), F32)
    c_ref[:, MLA_NOPE:] = jnp.cos(ang)
    s_ref[:, :MLA_NOPE] = jnp.zeros((tm, MLA_NOPE), F32)
    s_ref[:, MLA_NOPE:] = jnp.sin(ang) * sign_ref[...]


def rope_tables(positions, tm=512):
    t = positions.size
    inv_freq = ROPE_THETA ** (-jnp.arange(0, MLA_ROPE, 2, dtype=F32) / MLA_ROPE)
    width = MLA_QKP - MLA_NOPE
    pad = jnp.zeros((width - MLA_ROPE,), F32)
    freq = jnp.concatenate([inv_freq, inv_freq, pad]).reshape(1, width)
    half = MLA_ROPE // 2
    sign = jnp.concatenate([-jnp.ones((half,), F32), jnp.ones((half,), F32), pad]).reshape(1, width)
    return pl.pallas_call(
        _rope_kernel,
        grid=(t // tm,),
        in_specs=[pl.BlockSpec((tm, 1), lambda i: (i, 0)),
                  pl.BlockSpec((1, width), lambda i: (0, 0)),
                  pl.BlockSpec((1, width), lambda i: (0, 0))],
        out_specs=[pl.BlockSpec((tm, MLA_QKP), lambda i: (i, 0))] * 2,
        out_shape=[jax.ShapeDtypeStruct((t, MLA_QKP), F32)] * 2,
        compiler_params=_cparams(("arbitrary",)),
        name="rope_tables",
    )(positions.reshape(t, 1).astype(jnp.int32), freq, sign)


def _mla_prep_kernel(cq_ref, ckv_ref, kr_ref, krs_ref, cos_ref, sin_ref, qn_ref, kvn_ref,
                     wq_ref, wqs_ref, wkv_ref, q_out, k_out, v_out):
    def rms(x, g):
        return x * lax.rsqrt(jnp.mean(x * x, axis=-1, keepdims=True) + RMS_EPS) * g

    cqn = rms(cq_ref[...].astype(F32), qn_ref[...]).astype(BF16)
    ckvn = rms(ckv_ref[...].astype(F32), kvn_ref[...]).astype(BF16)
    cos = cos_ref[...]
    sin = sin_ref[...]
    kpe_t = (kr_ref[...] * cos[:, MLA_NOPE:] + krs_ref[...] * sin[:, MLA_NOPE:]).T.astype(k_out.dtype)
    q_all = _dot(cqn, wq_ref[...])
    qs_all = _dot(cqn, wqs_ref[...])
    kv_all = _dot(ckvn, wkv_ref[...])
    scale = (MLA_QK ** -0.5) * math.log2(math.e)
    for h in range(MLA_HEADS):
        cols = slice(h * MLA_QKP, (h + 1) * MLA_QKP)
        q_out[h] = ((q_all[:, cols] * cos + qs_all[:, cols] * sin) * scale).astype(q_out.dtype)
        base = h * (MLA_NOPE + MLA_V)
        k_out[h, :MLA_NOPE, :] = kv_all[:, base:base + MLA_NOPE].T.astype(k_out.dtype)
        k_out[h, MLA_NOPE:, :] = kpe_t
        v_out[h] = kv_all[:, base + MLA_NOPE:base + MLA_NOPE + MLA_V].astype(v_out.dtype)


def mla_prep(proj_a, proj_c, cos_t, sin_t, q_norm, kv_norm, w_uq, w_ukv, tm=1024):
    t = proj_a.shape[0]
    hh = MLA_HEADS
    wq = w_uq.reshape(MLA_Q_RANK, hh, MLA_QK)
    half = MLA_ROPE // 2
    zeros = lambda n: jnp.zeros((MLA_Q_RANK, hh, n), w_uq.dtype)
    padw = lambda w: w.reshape(MLA_Q_RANK, hh * MLA_QKP).astype(BF16)
    wq_pad = padw(jnp.concatenate([wq, zeros(MLA_QKP - MLA_QK)], axis=-1))
    wq_swap = padw(jnp.concatenate([zeros(MLA_NOPE), wq[:, :, MLA_NOPE + half:],
                                    wq[:, :, MLA_NOPE:MLA_NOPE + half], zeros(MLA_QKP - MLA_QK)], axis=-1))
    row = lambda v: v.reshape(1, -1).astype(F32)
    const = lambda i: (0, 0)
    heads_tile = lambda i: (0, i, 0)
    return pl.pallas_call(
        _mla_prep_kernel,
        grid=(t // tm,),
        in_specs=[pl.BlockSpec((tm, MLA_Q_RANK), lambda i: (i, PA_CQ // MLA_Q_RANK)),
                  pl.BlockSpec((tm, MLA_KV_RANK), lambda i: (i, PA_CKV // MLA_KV_RANK)),
                  pl.BlockSpec((tm, LANES), lambda i: (i, 1)),
                  pl.BlockSpec((tm, LANES), lambda i: (i, 2)),
                  pl.BlockSpec((tm, MLA_QKP), lambda i: (i, 0)),
                  pl.BlockSpec((tm, MLA_QKP), lambda i: (i, 0)),
                  pl.BlockSpec((1, MLA_Q_RANK), const),
                  pl.BlockSpec((1, MLA_KV_RANK), const),
                  pl.BlockSpec((MLA_Q_RANK, hh * MLA_QKP), const),
                  pl.BlockSpec((MLA_Q_RANK, hh * MLA_QKP), const),
                  pl.BlockSpec((MLA_KV_RANK, hh * (MLA_NOPE + MLA_V)), const)],
        out_specs=[pl.BlockSpec((hh, tm, MLA_QKP), heads_tile),
                   pl.BlockSpec((hh, MLA_QKP, tm), lambda i: (0, 0, i)),
                   pl.BlockSpec((hh, tm, MLA_V), heads_tile)],
        out_shape=[jax.ShapeDtypeStruct((hh, t, MLA_QKP), BF16),
                   jax.ShapeDtypeStruct((hh, MLA_QKP, t), BF16),
                   jax.ShapeDtypeStruct((hh, t, MLA_V), BF16)],
        compiler_params=_cparams(("arbitrary",)),
        name="mla_prep",
    )(proj_a, proj_a, proj_c, proj_c, cos_t, sin_t, row(q_norm), row(kv_norm),
      wq_pad, wq_swap, w_ukv.astype(BF16))


def _flash_kernel(qi_tab, ki_tab, q_ref, k_ref, v_ref, o_ref, sa_ref, sb_ref, m_ref, acc_ref, *, tq, tk, hb):
    p = pl.program_id(2)
    prev = jnp.maximum(p - 1, 0)
    qi = qi_tab[prev]
    ki = ki_tab[prev]
    last_k = (qi + 1) * (tq // tk) - 1
    even = p % 2 == 0
    needs_mask = (ki + 1) * tk > qi * tq + 1

    def update(masked, w_ref, r_ref):
        for h in range(hb):
            w_ref[h] = _dot(q_ref[h], k_ref[h])
        for h in range(hb):
            sc = r_ref[h]
            if masked:
                row = qi * tq + lax.broadcasted_iota(jnp.int32, (tq, tk), 0)
                col = ki * tk + lax.broadcasted_iota(jnp.int32, (tq, tk), 1)
                sc = jnp.where(col <= row, sc, -jnp.inf)
            v_ext = jnp.concatenate([v_ref[h], jnp.ones((tk, LANES), BF16)], axis=1)
            m_old = m_ref[h]
            m_new = jnp.maximum(m_old, jnp.max(sc, axis=-1, keepdims=True))
            alpha = jnp.exp2(m_old - m_new)
            pr = jnp.exp2(sc - m_new).astype(BF16)
            acc_ref[h] = alpha * acc_ref[h] + _dot(pr, v_ext)
            m_ref[h] = m_new

    @pl.when(p == 0)
    def _():
        for h in range(hb):
            sa_ref[h] = _dot(q_ref[h], k_ref[h])

    @pl.when(jnp.logical_and(p > 0, ki == 0))
    def _():
        m_ref[...] = jnp.full_like(m_ref, -jnp.inf)
        acc_ref[...] = jnp.zeros_like(acc_ref)

    for masked in (False, True):
        for is_even, w_ref, r_ref in ((True, sa_ref, sb_ref), (False, sb_ref, sa_ref)):
            cond = jnp.logical_and(p > 0, jnp.logical_and(needs_mask == masked, even == is_even))

            @pl.when(cond)
            def _(masked=masked, w_ref=w_ref, r_ref=r_ref):
                update(masked, w_ref, r_ref)

    @pl.when(jnp.logical_and(p > 0, ki == last_k))
    def _():
        for h in range(hb):
            acc = acc_ref[h]
            o_ref[:, h * MLA_V:(h + 1) * MLA_V] = (acc[:, :MLA_V] / acc[:, MLA_V:MLA_V + 1]).astype(o_ref.dtype)


def flash_attention(q, k, v, *, batch, seq, tq=1024, tk=1024, hb=2):
    nq = seq // tq
    nk = seq // tk
    pairs = [(a, b) for a in range(nq) for b in range((a + 1) * (tq // tk))]
    npairs = len(pairs)
    qi_tab = jnp.asarray([p[0] for p in pairs], jnp.int32)
    ki_tab = jnp.asarray([p[1] for p in pairs], jnp.int32)
    cur = lambda p: jnp.minimum(p, npairs - 1)
    prv = lambda p: jnp.maximum(p - 1, 0)
    scores = pltpu.VMEM((hb, tq, tk), F32)
    grid_spec = pltpu.PrefetchScalarGridSpec(
        num_scalar_prefetch=2,
        grid=(batch, MLA_HEADS // hb, npairs + 1),
        in_specs=[pl.BlockSpec((hb, tq, MLA_QKP), lambda b, h, p, qt, kt: (h, b * nq + qt[cur(p)], 0)),
                  pl.BlockSpec((hb, MLA_QKP, tk), lambda b, h, p, qt, kt: (h, 0, b * nk + kt[cur(p)])),
                  pl.BlockSpec((hb, tk, MLA_V), lambda b, h, p, qt, kt: (h, b * nk + kt[prv(p)], 0))],
        out_specs=pl.BlockSpec((tq, hb * MLA_V), lambda b, h, p, qt, kt: (b * nq + qt[prv(p)], h)),
        scratch_shapes=[scores, scores, pltpu.VMEM((hb, tq, 1), F32),
                        pltpu.VMEM((hb, tq, MLA_V + LANES), F32)],
    )
    return pl.pallas_call(
        functools.partial(_flash_kernel, tq=tq, tk=tk, hb=hb),
        grid_spec=grid_spec,
        out_shape=jax.ShapeDtypeStruct((batch * seq, MLA_HEADS * MLA_V), BF16),
        compiler_params=_cparams(("arbitrary", "arbitrary", "arbitrary")),
        name="flash_attention",
    )(qi_tab, ki_tab, q, k, v)


def _merge_kernel(ha_ref, hb_ref, hc_ref, hd_ref, ga_ref, gb_ref, gc_ref, gd_ref, w_ref, o_ref):
    acc = None
    for b, (h_ref, g_ref) in enumerate(((ha_ref, ga_ref), (hb_ref, gb_ref), (hc_ref, gc_ref), (hd_ref, gd_ref))):
        gate = 0.5 * jnp.tanh(0.5 * g_ref[...].astype(F32)) + 0.5
        term = gate * _dot(h_ref[...], w_ref[b])
        acc = term if acc is None else acc + term
    o_ref[...] = acc.astype(o_ref.dtype)


def merge_branches(h_a, h_b, h_c, h_d, proj_a, w_branch, tm=1024, tn=512):
    t = h_a.shape[0]
    width = h_a.shape[1]
    g0 = PA_GATE // tn
    per = D_MODEL // tn
    hspec = pl.BlockSpec((tm, width), lambda i, j: (i, 0))
    gspecs = [pl.BlockSpec((tm, tn), functools.partial(lambda i, j, b: (i, g0 + b * per + j), b=b))
              for b in range(N_BRANCH)]
    return pl.pallas_call(
        _merge_kernel,
        grid=(t // tm, D_MODEL // tn),
        in_specs=[hspec] * 4 + gspecs + [pl.BlockSpec((N_BRANCH, width, tn), lambda i, j: (0, 0, j))],
        out_specs=pl.BlockSpec((tm, tn), lambda i, j: (i, j)),
        out_shape=jax.ShapeDtypeStruct((t, D_MODEL), BF16),
        compiler_params=_cparams(("arbitrary", "arbitrary")),
        name="merge_branches",
    )(h_a, h_b, h_c, h_d, proj_a, proj_a, proj_a, proj_a, w_branch)


def _ffn_up_kernel(x_ref, wg_ref, wu_ref, cwg_ref, cwu_ref, bg_ref, bu_ref, o_ref, gwin_ref, uwin_ref, *, tm):
    s = pl.program_id(2)

    @pl.when(s == 0)
    def _():
        gwin_ref[0:HALO, :] = jnp.zeros((HALO, gwin_ref.shape[1]), F32)
        uwin_ref[0:HALO, :] = jnp.zeros((HALO, uwin_ref.shape[1]), F32)

    x = x_ref[...]
    gwin_ref[HALO:HALO + tm, :] = _dot(x, wg_ref[...])
    uwin_ref[HALO:HALO + tm, :] = _dot(x, wu_ref[...])
    off = HALO - (FFN_CONV - 1)

    def conv(win_ref, cw_ref, b_ref):
        acc = b_ref[...] + win_ref[off:off + tm, :] * cw_ref[0:1, :]
        for j in range(1, FFN_CONV):
            acc = acc + win_ref[off + j:off + j + tm, :] * cw_ref[j:j + 1, :]
        return acc

    gate = conv(gwin_ref, cwg_ref, bg_ref)
    up = conv(uwin_ref, cwu_ref, bu_ref)
    o_ref[...] = (_silu(gate) * up).astype(o_ref.dtype)
    gwin_ref[0:HALO, :] = gwin_ref[tm:tm + HALO, :]
    uwin_ref[0:HALO, :] = uwin_ref[tm:tm + HALO, :]


def ffn_up(hn, w_up, conv_w, conv_b, *, batch, seq, tm=1024, tn=512):
    ns = seq // tm
    wg = w_up[:, :FFN_DIM].astype(BF16)
    wu = w_up[:, FFN_DIM:].astype(BF16)
    padw = lambda w: jnp.zeros((HALO, FFN_DIM), F32).at[:FFN_CONV].set(w.astype(F32))
    row = lambda v: v.reshape(1, FFN_DIM).astype(F32)
    wspec = pl.BlockSpec((D_MODEL, tn), lambda j, b, s: (0, j))
    cspec = pl.BlockSpec((HALO, tn), lambda j, b, s: (0, j))
    bspec = pl.BlockSpec((1, tn), lambda j, b, s: (0, j))
    return pl.pallas_call(
        functools.partial(_ffn_up_kernel, tm=tm),
        grid=(FFN_DIM // tn, batch, ns),
        in_specs=[pl.BlockSpec((tm, D_MODEL), lambda j, b, s: (b * ns + s, 0)),
                  wspec, wspec, cspec, cspec, bspec, bspec],
        out_specs=pl.BlockSpec((tm, tn), lambda j, b, s: (b * ns + s, j)),
        out_shape=jax.ShapeDtypeStruct((batch * seq, FFN_DIM), BF16),
        scratch_shapes=[pltpu.VMEM((tm + HALO, tn), F32), pltpu.VMEM((tm + HALO, tn), F32)],
        compiler_params=_cparams(("arbitrary", "arbitrary", "arbitrary")),
        name="ffn_up",
    )(hn, wg, wu, padw(conv_w[:, :FFN_DIM]), padw(conv_w[:, FFN_DIM:]),
      row(conv_b[:FFN_DIM]), row(conv_b[FFN_DIM:]))


def _in_proj_weights(w_in):
    o = np.cumsum([0, 1024, 512, 512, 1024, 1024, 8, 8, 2048, 512, 512, 64, 8192])
    col = lambda i: w_in[:, int(o[i]):int(o[i + 1])]
    pool, q, k, v, z, a, b, conf, cq, ckv, krope, gates = (col(i) for i in range(12))
    w_a = jnp.concatenate([q, k, v, conf, pool, z, cq, ckv, gates], axis=1).astype(BF16)
    half = MLA_ROPE // 2
    zeros = lambda n: jnp.zeros((D_MODEL, n), w_in.dtype)
    w_c = jnp.concatenate([a, b, zeros(112), krope, zeros(64),
                           krope[:, half:], krope[:, :half], zeros(64)], axis=1).astype(BF16)
    return w_a, w_c


def kernel(x, positions, mix_norm, w_in, pool_w, pool_scale, gdn_conv_w, gdn_a_log, gdn_dt_bias, gdn_norm, conf_conv_w, conf_conv_b, conf_ln_g, conf_ln_b, mla_q_norm, mla_w_uq, mla_kv_norm, mla_w_ukv, w_pool_out, w_gdn_out, w_conf_out, w_mla_out, w_out, ffn_norm, ffn_w_up, ffn_conv_w, ffn_conv_b, ffn_w_down, final_norm):
    batch, seq, d = x.shape
    t = batch * seq
    xf = x.reshape(t, d)
    cos_t, sin_t = rope_tables(positions)
    for l in range(DEPTH):
        w_a, w_c = _in_proj_weights(w_in[l])
        proj_a, proj_c = in_proj(xf, mix_norm[l], w_a, w_c, tm=TILE_IN_PROJ[0], tn=TILE_IN_PROJ[1])
        h_a = pool_mixer(proj_a, pool_w[l], pool_scale[l], batch=batch, seq=seq)
        h_b = gated_deltanet(proj_a, proj_c, gdn_conv_w[l], gdn_a_log[l], gdn_dt_bias[l], gdn_norm[l],
                             batch=batch, seq=seq)
        h_c = conformer_conv(proj_a, conf_conv_w[l], conf_conv_b[l], conf_ln_g[l], conf_ln_b[l],
                             batch=batch, seq=seq)
        q, k, v = mla_prep(proj_a, proj_c, cos_t, sin_t, mla_q_norm[l], mla_kv_norm[l],
                           mla_w_uq[l], mla_w_ukv[l])
        h_d = flash_attention(q, k, v, batch=batch, seq=seq)
        w_branch = jnp.stack([w_pool_out[l], w_gdn_out[l], w_conf_out[l], w_mla_out[l]]).astype(BF16)
        merged = merge_branches(h_a, h_b, h_c, h_d, proj_a, w_branch, tm=TILE_MERGE[0], tn=TILE_MERGE[1])
        xf, hn = matmul(merged, w_out[l].astype(BF16), tm=TILE_OUT_PROJ[0], tn=TILE_OUT_PROJ[1], tk=D_MODEL,
                        out_dtype=F32, residual=xf, norm_gain=ffn_norm[l], norm_dtype=BF16, name="out_proj")
        mid = ffn_up(hn, ffn_w_up[l], ffn_conv_w[l], ffn_conv_b[l], batch=batch, seq=seq,
                     tm=TILE_FFN_UP[0], tn=TILE_FFN_UP[1])
        xf = matmul(mid, ffn_w_down[l].astype(BF16), tm=TILE_FFN_DOWN[0], tn=TILE_FFN_DOWN[1], tk=FFN_DIM,
                    out_dtype=F32, residual=xf, cols_outer=True, name="ffn_down")
    return rmsnorm(xf, final_norm, F32).reshape(batch, seq, d)
```
